```python
import jax, jax.numpy as jnp
from jax import lax
import numpy as np

D_MODEL = 4096
BATCH = 4
SEQ = 4096
DEPTH = 4

HEAD_DIM = 64
N_MIX_HEADS = D_MODEL // HEAD_DIM
RWKV_HEADS = N_MIX_HEADS // 2
RWKV_DIM = RWKV_HEADS * HEAD_DIM
RWKV_DECAY_LORA = 96
RWKV_AAA_LORA = 96
RWKV_GATE_LORA = 256
RWKV_LN_EPS = 64e-5
SWA_HEADS = N_MIX_HEADS - RWKV_HEADS
SWA_KV_HEADS = SWA_HEADS // 8
SWA_WINDOW = 128
DIL_HEADS = 24
DILATED_PAIRS = ((128, 1), (512, 4), (2048, 16))
FOX_HEADS = N_MIX_HEADS - DIL_HEADS
D_FF = ((8 * D_MODEL // 3 + 255) // 256) * 256

BLOCK = 128
RMS_EPS = 1e-6
NEG_INF = -1e30
N_EVEN = (DEPTH + 1) // 2
N_ODD = DEPTH // 2

RWKV_SPLIT = (RWKV_DIM, RWKV_DIM, RWKV_DIM, RWKV_DECAY_LORA, RWKV_AAA_LORA, RWKV_GATE_LORA)
RWKV_COLS = sum(RWKV_SPLIT)
SWA_SPLIT = (SWA_HEADS * HEAD_DIM, SWA_KV_HEADS * HEAD_DIM, SWA_KV_HEADS * HEAD_DIM)
EVEN_COLS = RWKV_COLS + sum(SWA_SPLIT)
ODD_SPLIT = (FOX_HEADS * HEAD_DIM,) * 3 + (FOX_HEADS,) + (DIL_HEADS * HEAD_DIM,) * 3
ODD_COLS = sum(ODD_SPLIT)

kernel_name = "hybrid_rwkv7_swa_sink_fox_dilated"


def rms_norm(x, g):
    xf = x.astype(jnp.float32)
    y = xf * lax.rsqrt(jnp.mean(xf * xf, axis=-1, keepdims=True) + RMS_EPS)
    return (y * g.astype(jnp.float32)).astype(x.dtype)


def split_cols(y, sizes):
    out, start = [], 0
    for s in sizes:
        out.append(y[..., start:start + s])
        start += s
    return out


def token_shift(y):
    return jnp.pad(y, ((0, 0), (1, 0), (0, 0)))[:, :-1]


def banded_attention(q, k, v, max_dist, sink=None):
    f32 = jnp.float32
    N, L, Hk, G, d = q.shape
    nb = -(-L // BLOCK)
    pad = nb * BLOCK - L
    if pad:
        q = jnp.pad(q, ((0, 0), (0, pad), (0, 0), (0, 0), (0, 0)))
        k = jnp.pad(k, ((0, 0), (0, pad), (0, 0), (0, 0)))
        v = jnp.pad(v, ((0, 0), (0, pad), (0, 0), (0, 0)))
    qb = q.reshape(N, nb, BLOCK, Hk, G, d)
    kb = k.reshape(N, nb, BLOCK, Hk, d)
    vb = v.reshape(N, nb, BLOCK, Hk, d)

    def with_prev(t):
        prev = jnp.pad(t, ((0, 0), (1, 0), (0, 0), (0, 0), (0, 0)))[:, :-1]
        return jnp.concatenate([prev, t], axis=2)

    kw, vw = with_prev(kb), with_prev(vb)
    s = jnp.einsum('nbqhgd,nbkhd->nbhgqk', qb, kw).astype(f32) * (d ** -0.5)
    dist = (jnp.arange(BLOCK)[:, None] + BLOCK) - jnp.arange(2 * BLOCK)[None, :]
    in_band = (dist >= 0) & (dist <= max_dist)
    has_prev = (jnp.arange(nb)[:, None, None] > 0) | (jnp.arange(2 * BLOCK)[None, None, :] >= BLOCK)
    mask = in_band[None] & has_prev
    s = jnp.where(mask[None, :, None, None], s, NEG_INF)
    m = jnp.max(s, axis=-1)
    if sink is not None:
        sink_f = sink.astype(f32)[None, None, :, :, None]
        m = jnp.maximum(m, sink_f)
    p = jnp.exp(s - m[..., None])
    den = jnp.sum(p, axis=-1)
    if sink is not None:
        den = den + jnp.exp(sink_f - m)
    den_t = den.transpose(0, 1, 4, 2, 3)
    o = jnp.einsum('nbhgqk,nbkhd->nbqhgd', p, vw.astype(f32)) / den_t[..., None]
    lse = (m + jnp.log(den)).transpose(0, 1, 4, 2, 3)
    o = o.reshape(N, nb * BLOCK, Hk, G, d)[:, :L]
    lse = lse.reshape(N, nb * BLOCK, Hk, G)[:, :L]
    return o, lse


def rwkv7_recurrence(r, w, k, v, a, b):
    Bn, T, H, N = r.shape

    def step(S, inp):
        r_t, w_t, k_t, v_t, a_t, b_t = inp
        sa = jnp.einsum('bhvk,bhk->bhv', S, a_t)
        S = S * w_t[:, :, None, :] + sa[..., None] * b_t[:, :, None, :] + v_t[..., None] * k_t[:, :, None, :]
        return S, jnp.einsum('bhvk,bhk->bhv', S, r_t)

    S0 = jnp.zeros((Bn, H, N, N), jnp.float32)
    xs = tuple(jnp.swapaxes(t, 0, 1) for t in (r, w, k, v, a, b))
    _, y = lax.scan(step, S0, xs)
    return jnp.swapaxes(y, 0, 1)


def forgetting_attention(q, k, v, log_f):
    f32 = jnp.float32
    Bn, T, H, d = q.shape
    nb = T // BLOCK
    c = jnp.cumsum(log_f, axis=1)
    c_keys = c.transpose(0, 2, 1)
    qb = q.reshape(Bn, nb, BLOCK, H, d).transpose(1, 0, 2, 3, 4)
    cb = c.reshape(Bn, nb, BLOCK, H).transpose(1, 0, 3, 2)
    kpos = jnp.arange(T)
    vf = v.astype(f32)

    def one_block(args):
        q_i, c_i, b_i = args
        s = jnp.einsum('bqhd,bkhd->bhqk', q_i, k).astype(f32) * (d ** -0.5)
        s = s + c_i[..., None] - c_keys[:, :, None, :]
        qpos = b_i * BLOCK + jnp.arange(BLOCK)
        s = jnp.where(kpos[None, :] <= qpos[:, None], s, NEG_INF)
        p = jax.nn.softmax(s, axis=-1)
        return jnp.einsum('bhqk,bkhd->bqhd', p, vf)

    out = lax.map(one_block, (qb, cb, jnp.arange(nb)))
    return out.transpose(1, 0, 2, 3, 4).reshape(Bn, T, H, d)


def dilated_mixture_attention(q, k, v):
    Bn, T, H, d = q.shape
    outs, lses = [], []
    for window, dil in DILATED_PAIRS:
        L = T // dil

        def to_streams(t):
            return t.reshape(Bn, L, dil, H, d).transpose(0, 2, 1, 3, 4).reshape(Bn * dil, L, H, d)

        o, lse = banded_attention(to_streams(q)[:, :, :, None, :], to_streams(k), to_streams(v), window // dil)
        outs.append(o.reshape(Bn, dil, L, H, d).transpose(0, 2, 1, 3, 4).reshape(Bn, T, H, d))
        lses.append(lse.reshape(Bn, dil, L, H).transpose(0, 2, 1, 3).reshape(Bn, T, H))
    wts = jax.nn.softmax(jnp.stack(lses), axis=0)
    return jnp.einsum('ibth,ibthd->bthd', wts, jnp.stack(outs))


def even_mixer(xn, w_in, mix, w0, w2, a0, a2, g2, k_k, k_a, r_k, ln_w, ln_b, sink, w_out):
    f32 = jnp.float32
    Bn, T, _ = xn.shape
    proj = xn @ w_in
    pa, pb = proj[..., :RWKV_COLS], proj[..., RWKV_COLS:]
    pa = (pa + (token_shift(pa) - pa) * mix).astype(f32)
    r, k, v, xw, xa, xg = split_cols(pa, RWKV_SPLIT)
    w_log = -jax.nn.softplus(-(w0.astype(f32) + jnp.tanh(xw) @ w2.astype(f32))) - 0.5
    decay = jnp.exp(-jnp.exp(w_log))
    a = jax.nn.sigmoid(a0.astype(f32) + xa @ a2.astype(f32))
    g = jax.nn.sigmoid(xg) @ g2.astype(f32)
    hsplit = lambda t: t.reshape(Bn, T, RWKV_HEADS, HEAD_DIM)
    kk = hsplit(k * k_k.astype(f32))
    kk = kk / jnp.maximum(jnp.linalg.norm(kk, axis=-1, keepdims=True), 1e-12)
    k = k * (1.0 + (a - 1.0) * k_a.astype(f32))
    rh, kh, vh, ah = hsplit(r), hsplit(k), hsplit(v), hsplit(a)
    y = rwkv7_recurrence(rh, hsplit(decay), kh, vh, -kk, kk * ah)
    mu = jnp.mean(y, axis=-1, keepdims=True)
    var = jnp.mean(jnp.square(y - mu), axis=-1, keepdims=True)
    y = ((y - mu) * lax.rsqrt(var + RWKV_LN_EPS)).reshape(Bn, T, RWKV_DIM)
    y = y * ln_w.astype(f32) + ln_b.astype(f32)
    bonus = jnp.sum(rh * kh * r_k.astype(f32), axis=-1, keepdims=True) * vh
    y_a = (y + bonus.reshape(Bn, T, RWKV_DIM)) * g
    qb_, kb_, vb_ = split_cols(pb, SWA_SPLIT)
    grp = SWA_HEADS // SWA_KV_HEADS
    o_b, _ = banded_attention(qb_.reshape(Bn, T, SWA_KV_HEADS, grp, HEAD_DIM),
                              kb_.reshape(Bn, T, SWA_KV_HEADS, HEAD_DIM),
                              vb_.reshape(Bn, T, SWA_KV_HEADS, HEAD_DIM),
                              SWA_WINDOW - 1, sink.reshape(SWA_KV_HEADS, grp))
    y_b = o_b.reshape(Bn, T, SWA_HEADS * HEAD_DIM)
    y = jnp.concatenate([y_a, y_b], axis=-1).astype(xn.dtype)
    return y @ w_out


def odd_mixer(xn, w_in, b_f, w_out):
    Bn, T, _ = xn.shape
    proj = xn @ w_in
    qc, kc, vc, fc, qd, kd, vd = split_cols(proj, ODD_SPLIT)
    hc = lambda t: t.reshape(Bn, T, FOX_HEADS, HEAD_DIM)
    hd = lambda t: t.reshape(Bn, T, DIL_HEADS, HEAD_DIM)
    log_f = jax.nn.log_sigmoid((fc + b_f).astype(jnp.float32))
    y_c = forgetting_attention(hc(qc), hc(kc), hc(vc), log_f)
    y_d = dilated_mixture_attention(hd(qd), hd(kd), hd(vd))
    y = jnp.concatenate([y_c.reshape(Bn, T, -1), y_d.reshape(Bn, T, -1)], axis=-1).astype(xn.dtype)
    return y @ w_out


def swiglu(x, w_gate, w_up, w_down):
    return (jax.nn.silu(x @ w_gate) * (x @ w_up)) @ w_down


def setup_inputs(seed: int = 0) -> dict:
    key = jax.random.key(seed)
    ks = jax.random.split(key, 32)
    f32 = jnp.float32

    def nrm(k, shape, fan_in, scale=1.0):
        return jax.random.normal(k, shape, f32) * (scale * fan_in ** -0.5)

    def near_one(k, shape, center=1.0):
        return center + 0.02 * jax.random.normal(k, shape, f32)

    NE, NO = N_EVEN, N_ODD
    return {
        "x": jax.random.normal(ks[0], (BATCH, SEQ, D_MODEL), f32),
        "norm_mix": near_one(ks[1], (DEPTH, D_MODEL)),
        "norm_ffn": near_one(ks[2], (DEPTH, D_MODEL)),
        "norm_final": near_one(ks[3], (D_MODEL,)),
        "ffn_w_gate": nrm(ks[4], (DEPTH, D_MODEL, D_FF), D_MODEL),
        "ffn_w_up": nrm(ks[5], (DEPTH, D_MODEL, D_FF), D_MODEL),
        "ffn_w_down": nrm(ks[6], (DEPTH, D_FF, D_MODEL), D_FF),
        "ev_w_in": nrm(ks[7], (NE, D_MODEL, EVEN_COLS), D_MODEL),
        "rwkv_mix": jax.random.uniform(ks[8], (NE, RWKV_COLS), f32),
        "rwkv_w0": jax.random.uniform(ks[9], (NE, RWKV_DIM), f32, -6.5, -1.5),
        "rwkv_w2": nrm(ks[10], (NE, RWKV_DECAY_LORA, RWKV_DIM), RWKV_DECAY_LORA, 0.5),
        "rwkv_a0": 0.1 * jax.random.normal(ks[11], (NE, RWKV_DIM), f32),
        "rwkv_a2": nrm(ks[12], (NE, RWKV_AAA_LORA, RWKV_DIM), RWKV_AAA_LORA, 0.5),
        "rwkv_g2": nrm(ks[13], (NE, RWKV_GATE_LORA, RWKV_DIM), RWKV_GATE_LORA),
        "rwkv_k_k": near_one(ks[14], (NE, RWKV_DIM), 0.85),
        "rwkv_k_a": near_one(ks[15], (NE, RWKV_DIM)),
        "rwkv_r_k": 0.1 * jax.random.normal(ks[16], (NE, RWKV_HEADS, HEAD_DIM), f32),
        "rwkv_ln_w": near_one(ks[17], (NE, RWKV_DIM)),
        "rwkv_ln_b": 0.02 * jax.random.normal(ks[18], (NE, RWKV_DIM), f32),
        "swa_sink": jax.random.normal(ks[19], (NE, SWA_HEADS), f32),
        "ev_w_out": nrm(ks[20], (NE, D_MODEL, D_MODEL), D_MODEL),
        "od_w_in": nrm(ks[21], (NO, D_MODEL, ODD_COLS), D_MODEL),
        "fox_b_f": jax.random.uniform(ks[22], (NO, FOX_HEADS), f32, 1.0, 4.0),
        "od_w_out": nrm(ks[23], (NO, D_MODEL, D_MODEL), D_MODEL),
    }


def reference(x, norm_mix, norm_ffn, norm_final, ffn_w_gate, ffn_w_up, ffn_w_down,
              ev_w_in, rwkv_mix, rwkv_w0, rwkv_w2, rwkv_a0, rwkv_a2, rwkv_g2, rwkv_k_k, rwkv_k_a,
              rwkv_r_k, rwkv_ln_w, rwkv_ln_b, swa_sink, ev_w_out, od_w_in, fox_b_f, od_w_out):
    h = x
    for layer in range(DEPTH):
        i = layer // 2
        xn = rms_norm(h, norm_mix[layer])
        if layer % 2 == 0:
            h = h + even_mixer(xn, ev_w_in[i], rwkv_mix[i], rwkv_w0[i], rwkv_w2[i], rwkv_a0[i],
                               rwkv_a2[i], rwkv_g2[i], rwkv_k_k[i], rwkv_k_a[i], rwkv_r_k[i],
                               rwkv_ln_w[i], rwkv_ln_b[i], swa_sink[i], ev_w_out[i])
        else:
            h = h + odd_mixer(xn, od_w_in[i], fox_b_f[i], od_w_out[i])
        h = h + swiglu(rms_norm(h, norm_ffn[layer]), ffn_w_gate[layer], ffn_w_up[layer], ffn_w_down[layer])
    return rms_norm(h, norm_final)
```

```python
import functools

import jax
import jax.numpy as jnp
from jax import lax
from jax.experimental import pallas as pl
from jax.experimental.pallas import tpu as pltpu

F32 = jnp.float32
BF16 = jnp.bfloat16
HIGHEST = lax.Precision.HIGHEST

HEAD_DIM = 64
PAIR = 2 * HEAD_DIM
BLOCK = 128
SWA_WINDOW = 128
DILATED_PAIRS = ((128, 1), (512, 4), (2048, 16))
RMS_EPS = 1e-6
RWKV_LN_EPS = 64e-5
NEG_INF = -1e30
RWKV_CHUNK = 64
LORA_PAD = 128
MIB = 1024 * 1024


def _params(semantics, vmem_mib):
    return pltpu.CompilerParams(dimension_semantics=semantics, vmem_limit_bytes=vmem_mib * MIB)


def _dot(a, b, precision=None):
    return jnp.dot(a, b, preferred_element_type=F32, precision=precision)


def _dot_bf16(a, b):
    return jnp.dot(a.astype(BF16), b.astype(BF16), preferred_element_type=F32)


def _dot_nt(a, b, precision=None):
    return lax.dot_general(a, b, (((1,), (1,)), ((), ())), preferred_element_type=F32, precision=precision)


def _sigmoid(x):
    return 1.0 / (1.0 + jnp.exp(-x))


def _softplus(x):
    return jnp.maximum(x, 0.0) + jnp.log(1.0 + jnp.exp(-jnp.abs(x)))


def _tile(n, target):
    t = min(n, target) // PAIR * PAIR
    while n % t:
        t -= PAIR
    return t


def _head_sum_matrix():
    r = lax.broadcasted_iota(jnp.int32, (PAIR, PAIR), 0) // HEAD_DIM
    c = lax.broadcasted_iota(jnp.int32, (PAIR, PAIR), 1) // HEAD_DIM
    return (r == c).astype(F32)


def _rmsnorm_kernel(x_ref, g_ref, o_ref):
    x = x_ref[...]
    ms = jnp.mean(x * x, axis=-1, keepdims=True)
    o_ref[...] = (x * lax.rsqrt(ms + RMS_EPS) * g_ref[...]).astype(o_ref.dtype)


def rmsnorm(x, g, out_dtype, tm=256):
    m, d = x.shape
    return pl.pallas_call(
        _rmsnorm_kernel,
        grid=(m // tm,),
        in_specs=[pl.BlockSpec((tm, d), lambda i: (i, 0)), pl.BlockSpec((1, d), lambda i: (0, 0))],
        out_specs=pl.BlockSpec((tm, d), lambda i: (i, 0)),
        out_shape=jax.ShapeDtypeStruct((m, d), out_dtype),
        compiler_params=_params(("parallel",), 40),
        name="rmsnorm",
    )(x, g.reshape(1, d))


def _matmul_sum_kernel(*refs, n_ops, has_res):
    a_refs, b_refs = refs[:n_ops], refs[n_ops:2 * n_ops]
    o_ref = refs[-1]
    acc = _dot(a_refs[0][...], b_refs[0][...])
    for a_ref, b_ref in zip(a_refs[1:], b_refs[1:]):
        acc = acc + _dot(a_ref[...], b_ref[...])
    if has_res:
        acc = acc + refs[2 * n_ops][...]
    o_ref[...] = acc.astype(o_ref.dtype)


def matmul_sum(a_ops, b_ops, n_out, out_dtype, tm, tn, residual=None, vmem_mib=48):
    m = a_ops[0][0].shape[0]
    tm, tn = _tile(m, tm), _tile(n_out, tn)
    in_specs, args = [], []
    for arr, kw, kb in a_ops:
        in_specs.append(pl.BlockSpec((tm, kw), lambda i, j, kb=kb: (i, kb)))
        args.append(arr)
    for arr, kw, kb in b_ops:
        in_specs.append(pl.BlockSpec((kw, tn), lambda i, j, kb=kb: (kb, j)))
        args.append(arr)
    if residual is not None:
        in_specs.append(pl.BlockSpec((tm, tn), lambda i, j: (i, j)))
        args.append(residual)
    return pl.pallas_call(
        functools.partial(_matmul_sum_kernel, n_ops=len(a_ops), has_res=residual is not None),
        grid=(m // tm, n_out // tn),
        in_specs=in_specs,
        out_specs=pl.BlockSpec((tm, tn), lambda i, j: (i, j)),
        out_shape=jax.ShapeDtypeStruct((m, n_out), out_dtype),
        compiler_params=_params(("parallel", "parallel"), vmem_mib),
        name="matmul_sum",
    )(*args)


def _swiglu_kernel(x_ref, wg_ref, wu_ref, o_ref):
    x = x_ref[...]
    gate = _dot(x, wg_ref[...])
    up = _dot(x, wu_ref[...])
    o_ref[...] = (gate * _sigmoid(gate) * up).astype(o_ref.dtype)


def swiglu_hidden(x, wg, wu, tm, tn):
    m, k = x.shape
    f = wg.shape[1]
    tm, tn = _tile(m, tm), _tile(f, tn)
    return pl.pallas_call(
        _swiglu_kernel,
        grid=(m // tm, f // tn),
        in_specs=[pl.BlockSpec((tm, k), lambda i, j: (i, 0)),
                  pl.BlockSpec((k, tn), lambda i, j: (0, j)),
                  pl.BlockSpec((k, tn), lambda i, j: (0, j))],
        out_specs=pl.BlockSpec((tm, tn), lambda i, j: (i, j)),
        out_shape=jax.ShapeDtypeStruct((m, f), BF16),
        compiler_params=_params(("parallel", "parallel"), 48),
        name="swiglu_hidden",
    )(x, wg, wu)


def _banded_kernel(*refs, n_pairs, rep, max_dist, has_sink, want_lse):
    q_ref, kc_ref, kp_ref, vc_ref, vp_ref = refs[:5]
    pos = 5
    sink_ref = None
    if has_sink:
        sink_ref = refs[pos]
        pos += 1
    o_ref = refs[pos]
    lse_ref = refs[pos + 1] if want_lse else None

    blk = q_ref.shape[0]
    i = pl.program_id(1)
    row = lax.broadcasted_iota(jnp.int32, (blk, 2 * blk), 0)
    col = lax.broadcasted_iota(jnp.int32, (blk, 2 * blk), 1)
    dist = row + blk - col
    mask = (dist >= 0) & (dist <= max_dist) & ((col >= blk) | (i > 0))
    head0 = lax.broadcasted_iota(jnp.int32, (1, PAIR), 1) < HEAD_DIM

    for p in range(n_pairs):
        ql = pl.ds(p * PAIR, PAIR)
        kl = pl.ds((p // rep) * PAIR, PAIR)
        q = q_ref[:, ql]
        k = jnp.concatenate([kp_ref[:, kl], kc_ref[:, kl]], axis=0)
        v = jnp.concatenate([vp_ref[:, kl], vc_ref[:, kl]], axis=0)
        probs, inv_den, lse = [], [], []
        for e in range(2):
            sel = head0 if e == 0 else jnp.logical_not(head0)
            s = _dot_nt(jnp.where(sel, q, jnp.zeros_like(q)), k)
            s = jnp.where(mask, s, NEG_INF)
            mx = jnp.max(s, axis=-1, keepdims=True)
            if has_sink:
                sk = sink_ref[0:1, pl.ds(p * PAIR + e * HEAD_DIM, 1)]
                mx = jnp.maximum(mx, sk)
            pe = jnp.exp(s - mx)
            den = jnp.sum(pe, axis=-1, keepdims=True)
            if has_sink:
                den = den + jnp.exp(sk - mx)
            probs.append(pe.astype(BF16))
            inv_den.append(1.0 / den)
            lse.append(mx + jnp.log(den))
        v_stack = jnp.concatenate([jnp.where(head0, v, jnp.zeros_like(v)),
                                   jnp.where(head0, jnp.zeros_like(v), v)], axis=0)
        o = _dot(jnp.concatenate(probs, axis=1), v_stack)
        o = o * jnp.where(head0, inv_den[0], inv_den[1])
        o_ref[:, ql] = o.astype(o_ref.dtype)
        if want_lse:
            lse_ref[:, ql] = jnp.where(head0, lse[0], lse[1])


def banded_attention(q_arr, k_arr, v_arr, q_col, k_col, v_col, n_streams, length, n_pairs, n_kv_pairs,
                     max_dist, out_dtype, sink=None, want_lse=False):
    nb = length // BLOCK
    rep = n_pairs // n_kv_pairs
    qw, kw = n_pairs * PAIR, n_kv_pairs * PAIR
    cur = lambda c: (lambda n, i: (n * nb + i, c))
    prev = lambda c: (lambda n, i: (n * nb + jnp.maximum(i - 1, 0), c))
    in_specs = [pl.BlockSpec((BLOCK, qw), cur(q_col)),
                pl.BlockSpec((BLOCK, kw), cur(k_col)), pl.BlockSpec((BLOCK, kw), prev(k_col)),
                pl.BlockSpec((BLOCK, kw), cur(v_col)), pl.BlockSpec((BLOCK, kw), prev(v_col))]
    args = [q_arr, k_arr, k_arr, v_arr, v_arr]
    if sink is not None:
        in_specs.append(pl.BlockSpec((1, qw), lambda n, i: (0, 0)))
        args.append(sink)
    rows = n_streams * length
    out_shape = [jax.ShapeDtypeStruct((rows, qw), out_dtype)]
    out_specs = [pl.BlockSpec((BLOCK, qw), lambda n, i: (n * nb + i, 0))]
    if want_lse:
        out_shape.append(jax.ShapeDtypeStruct((rows, qw), F32))
        out_specs.append(pl.BlockSpec((BLOCK, qw), lambda n, i: (n * nb + i, 0)))
    return pl.pallas_call(
        functools.partial(_banded_kernel, n_pairs=n_pairs, rep=rep, max_dist=max_dist,
                          has_sink=sink is not None, want_lse=want_lse),
        grid=(n_streams, nb),
        in_specs=in_specs, out_specs=out_specs, out_shape=out_shape,
        compiler_params=_params(("parallel", "parallel"), 32),
        name="banded_attention",
    )(*args)


def _dilated_merge_kernel(o1_ref, o2_ref, o3_ref, l1_ref, l2_ref, l3_ref, y_ref):
    l1, l2, l3 = l1_ref[...], l2_ref[...], l3_ref[...]
    mx = jnp.maximum(jnp.maximum(l1, l2), l3)
    e1, e2, e3 = jnp.exp(l1 - mx), jnp.exp(l2 - mx), jnp.exp(l3 - mx)
    num = e1 * o1_ref[...] + e2 * o2_ref[...] + e3 * o3_ref[...]
    y_ref[...] = (num / (e1 + e2 + e3)).astype(y_ref.dtype)


def dilated_merge(outs, lses, tm=256):
    m, w = outs[0].shape
    spec = pl.BlockSpec((tm, w), lambda i: (i, 0))
    return pl.pallas_call(
        _dilated_merge_kernel,
        grid=(m // tm,),
        in_specs=[spec] * 6, out_specs=spec,
        out_shape=jax.ShapeDtypeStruct((m, w), BF16),
        compiler_params=_params(("parallel",), 40),
        name="dilated_merge",
    )(*outs, *lses)


def _forget_cumsum_kernel(f_ref, b_ref, c_ref, carry_ref):
    @pl.when(pl.program_id(1) == 0)
    def _():
        carry_ref[...] = jnp.zeros_like(carry_ref)

    blk = f_ref.shape[0]
    log_f = -_softplus(-(f_ref[...] + b_ref[...]))
    tri = (lax.broadcasted_iota(jnp.int32, (blk, blk), 0) >= lax.broadcasted_iota(jnp.int32, (blk, blk), 1))
    c = _dot(tri.astype(F32), log_f, HIGHEST) + carry_ref[...]
    c_ref[...] = c
    carry_ref[...] = c[blk - 1:blk, :]


def forget_cumsum(fc, b_f, batch, seq, blk=128):
    w = fc.shape[1]
    nb = seq // blk
    return pl.pallas_call(
        _forget_cumsum_kernel,
        grid=(batch, nb),
        in_specs=[pl.BlockSpec((blk, w), lambda b, i: (b * nb + i, 0)), pl.BlockSpec((1, w), lambda b, i: (0, 0))],
        out_specs=pl.BlockSpec((blk, w), lambda b, i: (b * nb + i, 0)),
        out_shape=jax.ShapeDtypeStruct(fc.shape, F32),
        scratch_shapes=[pltpu.VMEM((1, w), F32)],
        compiler_params=_params(("parallel", "arbitrary"), 16),
        name="forget_cumsum",
    )(fc, b_f)


def _fox_kernel(q_ref, k_ref, v_ref, cq_ref, ck_ref, o_ref, m_ref, l_ref, acc_ref):
    i, j = pl.program_id(2), pl.program_id(3)
    tq, tk = q_ref.shape[0], k_ref.shape[0]
    head0 = lax.broadcasted_iota(jnp.int32, (1, PAIR), 1) < HEAD_DIM

    @pl.when(j == 0)
    def _():
        m_ref[...] = jnp.full_like(m_ref, NEG_INF)
        l_ref[...] = jnp.zeros_like(l_ref)
        acc_ref[...] = jnp.zeros_like(acc_ref)

    @pl.when(j <= i)
    def _():
        q, k, v = q_ref[...], k_ref[...], v_ref[...]
        row = lax.broadcasted_iota(jnp.int32, (tq, tk), 0) + i * tq
        col = lax.broadcasted_iota(jnp.int32, (tq, tk), 1) + j * tk
        causal = col <= row
        probs, alphas = [], []
        for e in range(2):
            sel = head0 if e == 0 else jnp.logical_not(head0)
            s = _dot_nt(jnp.where(sel, q, jnp.zeros_like(q)), k)
            s = s + cq_ref[0, e] - ck_ref[0, e]
            s = jnp.where(causal, s, NEG_INF)
            m_prev = m_ref[e]
            m_new = jnp.maximum(m_prev, jnp.max(s, axis=-1, keepdims=True))
            pe = jnp.exp(s - m_new)
            alpha = jnp.exp(m_prev - m_new)
            l_ref[e] = alpha * l_ref[e] + jnp.sum(pe, axis=-1, keepdims=True)
            m_ref[e] = m_new
            probs.append(pe.astype(BF16))
            alphas.append(alpha)
        v_stack = jnp.concatenate([jnp.where(head0, v, jnp.zeros_like(v)),
                                   jnp.where(head0, jnp.zeros_like(v), v)], axis=0)
        pv = _dot(jnp.concatenate(probs, axis=1), v_stack)
        acc_ref[...] = acc_ref[...] * jnp.where(head0, alphas[0], alphas[1]) + pv

    @pl.when(j == pl.num_programs(3) - 1)
    def _():
        o_ref[...] = (acc_ref[...] / jnp.where(head0, l_ref[0], l_ref[1])).astype(o_ref.dtype)


def fox_attention(qkv, c_col, c_row, batch, seq, n_pairs, q_col, k_col, v_col, tq):
    nq = seq // tq
    kv = lambda base: (lambda b, p, i, j: (b * nq + jnp.minimum(j, i), base + p))
    return pl.pallas_call(
        _fox_kernel,
        grid=(batch, n_pairs, nq, nq),
        in_specs=[pl.BlockSpec((tq, PAIR), lambda b, p, i, j: (b * nq + i, q_col + p)),
                  pl.BlockSpec((tq, PAIR), kv(k_col)),
                  pl.BlockSpec((tq, PAIR), kv(v_col)),
                  pl.BlockSpec((1, 2, tq, 1), lambda b, p, i, j: (b, p, i, 0)),
                  pl.BlockSpec((1, 2, 1, tq), lambda b, p, i, j: (b, p, 0, jnp.minimum(j, i)))],
        out_specs=pl.BlockSpec((tq, PAIR), lambda b, p, i, j: (b * nq + i, p)),
        out_shape=jax.ShapeDtypeStruct((batch * seq, n_pairs * PAIR), BF16),
        scratch_shapes=[pltpu.VMEM((2, tq, 1), F32), pltpu.VMEM((2, tq, 1), F32), pltpu.VMEM((tq, PAIR), F32)],
        compiler_params=_params(("parallel", "parallel", "parallel", "arbitrary"), 32),
        name="fox_attention",
    )(qkv, qkv, qkv, c_col, c_row)


def _rwkv_prep_kernel(x_ref, xp_ref, mix_ref, w0_ref, w2_ref, a0_ref, a2_ref, g2_ref, kk_ref, ka_ref,
                      r_ref, lw_ref, k_ref, v_ref, a_ref, b_ref, g_ref, *, dim):
    first = pl.program_id(1) == 0
    x = x_ref[...]
    tt = x.shape[0]
    prev_row = jnp.where(first, 0.0, xp_ref[7:8, :])
    row = lax.broadcasted_iota(jnp.int32, (tt, 1), 0)
    shifted = jnp.where(row == 0, prev_row, pltpu.roll(x, 1, 0))
    x = x + (shifted - x) * mix_ref[...]

    r, k, v = x[:, 0:dim], x[:, dim:2 * dim], x[:, 2 * dim:3 * dim]
    o = 3 * dim
    xw, xa, xg = x[:, o:o + LORA_PAD], x[:, o + LORA_PAD:o + 2 * LORA_PAD], x[:, o + 2 * LORA_PAD:]
    w_log = -_softplus(-(w0_ref[...] + _dot(jnp.tanh(xw), w2_ref[...], HIGHEST))) - 0.5
    a = _sigmoid(a0_ref[...] + _dot(xa, a2_ref[...], HIGHEST))
    g = _dot(_sigmoid(xg), g2_ref[...], HIGHEST)

    kk = k * kk_ref[...]
    ones = _head_sum_matrix()
    norm = []
    for p in range(dim // PAIR):
        kp = kk[:, p * PAIR:(p + 1) * PAIR]
        norm.append(jnp.sqrt(_dot(kp * kp, ones, HIGHEST)))
    kk = kk / jnp.maximum(jnp.concatenate(norm, axis=1), 1e-12)

    r_ref[...] = r
    lw_ref[...] = -jnp.exp(w_log)
    k_ref[...] = k * (1.0 + (a - 1.0) * ka_ref[...])
    v_ref[...] = v
    a_ref[...] = -kk
    b_ref[...] = kk * a
    g_ref[...] = g


def rwkv_prep(pa, mix, w0, w2, a0, a2, g2, k_k, k_a, batch, seq, dim, tt=128):
    m, cols = pa.shape
    nt = seq // tt
    full = lambda arr: pl.BlockSpec(arr.shape, lambda b, i: (0, 0))
    out_spec = pl.BlockSpec((tt, dim), lambda b, i: (b * nt + i, 0))
    consts = [mix, w0, w2, a0, a2, g2, k_k, k_a]
    return pl.pallas_call(
        functools.partial(_rwkv_prep_kernel, dim=dim),
        grid=(batch, nt),
        in_specs=[pl.BlockSpec((tt, cols), lambda b, i: (b * nt + i, 0)),
                  pl.BlockSpec((8, cols), lambda b, i: (jnp.maximum((b * nt + i) * (tt // 8) - 1, 0), 0))]
                 + [full(c) for c in consts],
        out_specs=[out_spec] * 7,
        out_shape=[jax.ShapeDtypeStruct((m, dim), F32)] * 7,
        compiler_params=_params(("parallel", "parallel"), 48),
        name="rwkv_prep",
    )(pa, pa, *consts)


def _stack_heads(x, head0):
    zero = jnp.zeros_like(x)
    return jnp.concatenate([jnp.where(head0, x, zero), jnp.where(head0, zero, x)], axis=0)


def _rwkv_scan_kernel(r_ref, lw_ref, k_ref, v_ref, a_ref, b_ref, g_ref, rk_ref, lnw_ref, lnb_ref,
                      o_ref, s_ref, y_ref):
    c_len = RWKV_CHUNK
    tt = r_ref.shape[0]

    @pl.when(pl.program_id(2) == 0)
    def _():
        s_ref[...] = jnp.zeros_like(s_ref)

    head0 = lax.broadcasted_iota(jnp.int32, (1, PAIR), 1) < HEAD_DIM
    n2 = 2 * c_len
    row = lax.broadcasted_iota(jnp.int32, (n2, n2), 0)
    col = lax.broadcasted_iota(jnp.int32, (n2, n2), 1)
    t_loc, j_loc = row % c_len, col % c_len
    same_head = (row // c_len) == (col // c_len)
    strict = same_head & (j_loc < t_loc)
    incl = same_head & (j_loc <= t_loc)
    sub_block = same_head & ((t_loc // 16) == (j_loc // 16))
    eye = (row == col).astype(F32)
    tri = (lax.broadcasted_iota(jnp.int32, (c_len, c_len), 0)
           >= lax.broadcasted_iota(jnp.int32, (c_len, c_len), 1)).astype(F32)
    zeros = jnp.zeros((n2, PAIR), F32)

    for c in range(tt // c_len):
        sl = pl.ds(c * c_len, c_len)
        r, lw, k, v, a, b = r_ref[sl, :], lw_ref[sl, :], k_ref[sl, :], v_ref[sl, :], a_ref[sl, :], b_ref[sl, :]
        cw = _dot(tri, lw, HIGHEST)
        last = cw[c_len - 1:c_len, :]
        e_pos, e_neg = jnp.exp(cw), jnp.exp(-cw)
        e_end = jnp.exp(last) * e_neg
        a_s = _stack_heads(a * jnp.exp(cw - lw), head0)
        r_s = _stack_heads(r * e_pos, head0)
        b_s = _stack_heads(b * e_neg, head0)
        k_s = _stack_heads(k * e_neg, head0)
        bend_s = _stack_heads(b * e_end, head0)
        kend_s = _stack_heads(k * e_end, head0)
        v_s = _stack_heads(v, head0)

        scores = _dot_nt(jnp.concatenate([a_s, r_s], axis=0).astype(BF16),
                         jnp.concatenate([b_s, k_s], axis=0).astype(BF16))
        l_ab = jnp.where(strict, scores[:n2, :n2], 0.0)
        l_ak = jnp.where(strict, scores[:n2, n2:], 0.0)
        m_rb = jnp.where(incl, scores[n2:, :n2], 0.0)
        m_rk = jnp.where(incl, scores[n2:, n2:], 0.0)

        d1 = jnp.where(sub_block, l_ab, 0.0)
        rest = l_ab - d1
        d2 = _dot(d1, d1, HIGHEST)
        d4 = _dot(d2, d2, HIGHEST)
        d8 = _dot(d4, d4, HIGHEST)
        t_diag = _dot(_dot(eye + d1, eye + d2, HIGHEST), _dot(eye + d4, eye + d8, HIGHEST), HIGHEST)
        n1 = _dot(t_diag, rest, HIGHEST)
        n_sq = _dot(n1, n1, HIGHEST)
        t_inv = _dot(_dot(eye + n1, eye + n_sq, HIGHEST), t_diag, HIGHEST)

        lv = _dot_bf16(l_ak, v_s)
        au = _dot_bf16(t_inv, jnp.concatenate([a_s, lv], axis=1))
        lhs = jnp.concatenate([jnp.concatenate([m_rb, m_rk], axis=1),
                               jnp.concatenate([bend_s.T, kend_s.T], axis=1)], axis=0)
        rhs = jnp.concatenate([au, jnp.concatenate([zeros, v_s], axis=1)], axis=0)
        z = _dot_bf16(lhs, rhs)
        r_hat = r_s + z[:n2, :PAIR]
        y_hat = z[:n2, PAIR:]
        p_mat = z[n2:, :PAIR] + eye * jnp.exp(last)
        q_mat = z[n2:, PAIR:]

        state = s_ref[...]
        y_ref[sl, :] = (_dot(r_hat[:c_len] + r_hat[c_len:], state, HIGHEST) + y_hat[:c_len] + y_hat[c_len:])
        s_ref[...] = _dot(p_mat, state, HIGHEST) + q_mat

    ones = _head_sum_matrix()
    y = y_ref[...]
    mu = _dot(y, ones, HIGHEST) * (1.0 / HEAD_DIM)
    d = y - mu
    var = _dot(d * d, ones, HIGHEST) * (1.0 / HEAD_DIM)
    yn = d * lax.rsqrt(var + RWKV_LN_EPS) * lnw_ref[...] + lnb_ref[...]
    r, k, v = r_ref[...], k_ref[...], v_ref[...]
    bonus = _dot(r * k * rk_ref[...], ones, HIGHEST) * v
    o_ref[...] = ((yn + bonus) * g_ref[...]).astype(o_ref.dtype)


def rwkv_scan(r, lw, k, v, a, b, g, r_k, ln_w, ln_b, batch, seq, tt=256):
    m, dim = r.shape
    n_pairs = dim // PAIR
    nt = seq // tt
    seq_spec = pl.BlockSpec((tt, PAIR), lambda bb, p, i: (bb * nt + i, p))
    vec_spec = pl.BlockSpec((1, PAIR), lambda bb, p, i: (0, p))
    return pl.pallas_call(
        _rwkv_scan_kernel,
        grid=(batch, n_pairs, nt),
        in_specs=[seq_spec] * 7 + [vec_spec] * 3,
        out_specs=seq_spec,
        out_shape=jax.ShapeDtypeStruct((m, dim), BF16),
        scratch_shapes=[pltpu.VMEM((PAIR, PAIR), F32), pltpu.VMEM((tt, PAIR), F32)],
        compiler_params=_params(("parallel", "parallel", "arbitrary"), 32),
        name="rwkv_scan",
    )(r, lw, k, v, a, b, g, r_k, ln_w, ln_b)


def _pad_cols(w, width):
    return jnp.pad(w, ((0, 0), (0, width - w.shape[1])))


def _pad_rows(w, height):
    return jnp.pad(w, ((0, height - w.shape[0]), (0, 0)))


def _dup_kv_heads(w):
    d, c = w.shape
    return jnp.tile(w.reshape(d, c // HEAD_DIM, 1, HEAD_DIM), (1, 1, 2, 1)).reshape(d, 2 * c)


def even_mixer(h, xn, batch, seq, w_in, mix, w0, w2, a0, a2, g2, k_k, k_a, r_k, ln_w, ln_b, sink, w_out,
               dim, swa_heads, swa_kv_heads, tm, tn):
    d_model = xn.shape[1]
    decay_lora, aaa_lora = w2.shape[0], a2.shape[0]
    o = 3 * dim
    o_a, o_g = o + decay_lora, o + decay_lora + aaa_lora
    rwkv_cols = o_g + g2.shape[0]
    scale = HEAD_DIM ** -0.5

    def rwkv_layout(t):
        return jnp.concatenate([t[..., :o], _pad_cols(t[..., o:o_a], LORA_PAD), _pad_cols(t[..., o_a:o_g], LORA_PAD),
                                t[..., o_g:rwkv_cols]], axis=-1)

    w_a = rwkv_layout(w_in).astype(BF16)
    q_end = rwkv_cols + swa_heads * HEAD_DIM
    k_end = q_end + swa_kv_heads * HEAD_DIM
    w_b = jnp.concatenate([w_in[:, rwkv_cols:q_end] * scale, _dup_kv_heads(w_in[:, q_end:k_end]),
                           _dup_kv_heads(w_in[:, k_end:])], axis=1).astype(BF16)

    pa = matmul_sum([(xn, d_model, 0)], [(w_a, d_model, 0)], w_a.shape[1], F32, tm, tn)
    pb = matmul_sum([(xn, d_model, 0)], [(w_b, d_model, 0)], w_b.shape[1], BF16, tm, tn)

    row = lambda t: t.reshape(1, -1)
    prep = rwkv_prep(pa, rwkv_layout(row(mix)), row(w0), _pad_rows(w2, LORA_PAD), row(a0), _pad_rows(a2, LORA_PAD),
                     g2, row(k_k), row(k_a), batch, seq, dim)
    y_a = rwkv_scan(*prep, row(r_k), row(ln_w), row(ln_b), batch, seq)

    q_pairs, kv_pairs = swa_heads // 2, swa_kv_heads
    kv_w = kv_pairs * PAIR
    kv_base = (q_pairs * PAIR) // kv_w
    sink_lanes = jnp.repeat(sink, HEAD_DIM).reshape(1, -1)
    (y_b,) = banded_attention(pb, pb, pb, 0, kv_base, kv_base + 1, batch, seq, q_pairs, kv_pairs,
                              SWA_WINDOW - 1, BF16, sink=sink_lanes)

    w_ya, w_yb = w_out[:dim].astype(BF16), w_out[dim:].astype(BF16)
    return matmul_sum([(y_a, dim, 0), (y_b, d_model - dim, 0)], [(w_ya, dim, 0), (w_yb, d_model - dim, 0)],
                      d_model, F32, tm, tn, residual=h)


def odd_mixer(h, xn, batch, seq, w_in, b_f, w_out, fox_heads, dil_heads, tm, tn, fox_tq):
    d_model = xn.shape[1]
    cw, dw = fox_heads * HEAD_DIM, dil_heads * HEAD_DIM
    scale = HEAD_DIM ** -0.5
    f0 = 3 * cw
    d0 = f0 + fox_heads
    w_main = jnp.concatenate([w_in[:, :cw] * scale, w_in[:, cw:f0], w_in[:, d0:d0 + dw] * scale,
                              w_in[:, d0 + dw:]], axis=1).astype(BF16)
    w_f = _pad_cols(w_in[:, f0:d0], PAIR).astype(BF16)
    proj = matmul_sum([(xn, d_model, 0)], [(w_main, d_model, 0)], w_main.shape[1], BF16, tm, tn)
    fc = matmul_sum([(xn, d_model, 0)], [(w_f, d_model, 0)], PAIR, F32, tm, PAIR)

    c = forget_cumsum(fc, _pad_cols(b_f.reshape(1, -1), PAIR), batch, seq)
    c = c[:, :fox_heads].reshape(batch, seq, fox_heads).transpose(0, 2, 1)
    fox_pairs = fox_heads // 2
    y_c = fox_attention(proj, c[..., None], c[:, :, None, :], batch, seq, fox_pairs, 0, fox_pairs, 2 * fox_pairs,
                        fox_tq)

    dil_pairs = dil_heads // 2
    qkv_d = proj[:, f0:]
    outs, lses = [], []
    for window, dil in DILATED_PAIRS:
        length = seq // dil
        if dil == 1:
            base = f0 // dw
            o, lse = banded_attention(proj, proj, proj, base, base + 1, base + 2, batch, seq, dil_pairs, dil_pairs,
                                      window // dil, F32, want_lse=True)
        else:
            streams = qkv_d.reshape(batch, length, dil, 3 * dw).transpose(0, 2, 1, 3).reshape(batch * seq, 3 * dw)
            o, lse = banded_attention(streams, streams, streams, 0, 1, 2, batch * dil, length, dil_pairs, dil_pairs,
                                      window // dil, F32, want_lse=True)
            unstream = lambda t: t.reshape(batch, dil, length, dw).transpose(0, 2, 1, 3).reshape(batch * seq, dw)
            o, lse = unstream(o), unstream(lse)
        outs.append(o)
        lses.append(lse)
    y_d = dilated_merge(outs, lses)

    w_c, w_d = w_out[:cw].astype(BF16), w_out[cw:].astype(BF16)
    return matmul_sum([(y_c, cw, 0), (y_d, dw, 0)], [(w_c, cw, 0), (w_d, dw, 0)], d_model, F32, tm, tn, residual=h)


def ffn(h, xn, w_gate, w_up, w_down, tm, tn_hidden, tm_down, tn_down):
    d_ff = w_gate.shape[1]
    hidden = swiglu_hidden(xn, w_gate.astype(BF16), w_up.astype(BF16), tm, tn_hidden)
    half = d_ff // 2
    w_d = w_down.astype(BF16)
    return matmul_sum([(hidden, half, 0), (hidden, half, 1)], [(w_d, half, 0), (w_d, half, 1)],
                      w_down.shape[1], F32, tm_down, tn_down, residual=h, vmem_mib=56)


def kernel(x, norm_mix, norm_ffn, norm_final, ffn_w_gate, ffn_w_up, ffn_w_down, ev_w_in, rwkv_mix, rwkv_w0, rwkv_w2, rwkv_a0, rwkv_a2, rwkv_g2, rwkv_k_k, rwkv_k_a, rwkv_r_k, rwkv_ln_w, rwkv_ln_b, swa_sink, ev_w_out, od_w_in, fox_b_f, od_w_out):
    batch, seq, d_model = x.shape
    depth = norm_mix.shape[0]
    dim = rwkv_w0.shape[1]
    swa_heads = swa_sink.shape[1]
    fox_heads = fox_b_f.shape[1]
    dil_heads = d_model // HEAD_DIM - fox_heads
    tm, tn = 1024, 512
    h = x.reshape(batch * seq, d_model)
    for layer in range(depth):
        i = layer // 2
        xn = rmsnorm(h, norm_mix[layer], BF16)
        if layer % 2 == 0:
            h = even_mixer(h, xn, batch, seq, ev_w_in[i], rwkv_mix[i], rwkv_w0[i], rwkv_w2[i], rwkv_a0[i], rwkv_a2[i],
                           rwkv_g2[i], rwkv_k_k[i], rwkv_k_a[i], rwkv_r_k[i].reshape(-1), rwkv_ln_w[i], rwkv_ln_b[i],
                           swa_sink[i], ev_w_out[i], dim, swa_heads, swa_heads // 8, tm, tn)
        else:
            h = odd_mixer(h, xn, batch, seq, od_w_in[i], fox_b_f[i], od_w_out[i], fox_heads, dil_heads, tm, tn, 512)
        xn = rmsnorm(h, norm_ffn[layer], BF16)
        h = ffn(h, xn, ffn_w_gate[layer], ffn_w_up[layer], ffn_w_down[layer], tm, 256, 512, 256)
    return rmsnorm(h, norm_final, F32).reshape(batch, seq, d_model)
```

```python
import functools
import math

import jax
import jax.numpy as jnp
from jax import lax
from jax.experimental import pallas as pl
from jax.experimental.pallas import tpu as pltpu

F32 = jnp.float32
BF16 = jnp.bfloat16
HIGHEST = lax.Precision.HIGHEST

HEAD_DIM = 64
PAIR = 2 * HEAD_DIM
BLOCK = 128
SWA_WINDOW = 128
DILATED_PAIRS = ((128, 1), (512, 4), (2048, 16))
RMS_EPS = 1e-6
RWKV_LN_EPS = 64e-5
NEG_INF = -1e30
RWKV_CHUNK = 64
LORA_PAD = 128
MIB = 1024 * 1024


def _params(semantics, vmem_mib):
    return pltpu.CompilerParams(dimension_semantics=semantics, vmem_limit_bytes=vmem_mib * MIB)


def _dot(a, b, precision=None):
    return jnp.dot(a, b, preferred_element_type=F32, precision=precision)


def _dot_bf16(a, b):
    return jnp.dot(a.astype(BF16), b.astype(BF16), preferred_element_type=F32)


def _dot_nt(a, b, precision=None):
    return lax.dot_general(a, b, (((1,), (1,)), ((), ())), preferred_element_type=F32, precision=precision)


def _split3(x):
    hi = x.astype(BF16)
    rem = x - hi.astype(F32)
    mid = rem.astype(BF16)
    lo = (rem - mid.astype(F32)).astype(BF16)
    return hi, mid, lo


def _dot_exact_lhs(m, x):
    hi, mid, lo = _split3(x)
    return _dot(m, lo) + _dot(m, mid) + _dot(m, hi)


def _dot_exact_rhs(x, m):
    hi, mid, lo = _split3(x)
    return _dot(lo, m) + _dot(mid, m) + _dot(hi, m)


def _dot_x3(a, b):
    a_hi, b_hi = a.astype(BF16), b.astype(BF16)
    a_lo, b_lo = (a - a_hi.astype(F32)).astype(BF16), (b - b_hi.astype(F32)).astype(BF16)
    return _dot(a_lo, b_hi) + _dot(a_hi, b_lo) + _dot(a_hi, b_hi)


def _sigmoid(x):
    return 1.0 / (1.0 + jnp.exp(-x))


def _softplus(x):
    return jnp.maximum(x, 0.0) + jnp.log(1.0 + jnp.exp(-jnp.abs(x)))


def _tile(n, target):
    t = min(n, target) // PAIR * PAIR
    while n % t:
        t -= PAIR
    return t


def _head_sum_matrix():
    r = lax.broadcasted_iota(jnp.int32, (PAIR, PAIR), 0) // HEAD_DIM
    c = lax.broadcasted_iota(jnp.int32, (PAIR, PAIR), 1) // HEAD_DIM
    return (r == c).astype(F32)


def _rmsnorm_kernel(x_ref, g_ref, o_ref):
    x = x_ref[...]
    ms = jnp.mean(x * x, axis=-1, keepdims=True)
    o_ref[...] = (x * lax.rsqrt(ms + RMS_EPS) * g_ref[...]).astype(o_ref.dtype)


def rmsnorm(x, g, out_dtype, tm=256):
    m, d = x.shape
    return pl.pallas_call(
        _rmsnorm_kernel,
        grid=(m // tm,),
        in_specs=[pl.BlockSpec((tm, d), lambda i: (i, 0)), pl.BlockSpec((1, d), lambda i: (0, 0))],
        out_specs=pl.BlockSpec((tm, d), lambda i: (i, 0)),
        out_shape=jax.ShapeDtypeStruct((m, d), out_dtype),
        compiler_params=_params(("parallel",), 40),
        name="rmsnorm",
    )(x, g.reshape(1, d))


def _matmul_sum_kernel(*refs, n_ops, has_res):
    a_refs, b_refs = refs[:n_ops], refs[n_ops:2 * n_ops]
    o_ref = refs[-1]
    acc = _dot(a_refs[0][...], b_refs[0][...])
    for a_ref, b_ref in zip(a_refs[1:], b_refs[1:]):
        acc = acc + _dot(a_ref[...], b_ref[...])
    if has_res:
        acc = acc + refs[2 * n_ops][...]
    o_ref[...] = acc.astype(o_ref.dtype)


def matmul_sum(a_ops, b_ops, n_out, out_dtype, tm, tn, residual=None, layer=None, vmem_mib=48):
    m = a_ops[0][0].shape[0]
    tm, tn = _tile(m, tm), _tile(n_out, tn)
    in_specs, args = [], []
    for arr, kw, kb in a_ops:
        in_specs.append(pl.BlockSpec((tm, kw), lambda i, j, kb=kb: (i, kb)))
        args.append(arr)
    for arr, kw, kb in b_ops:
        if layer is None:
            in_specs.append(pl.BlockSpec((kw, tn), lambda i, j, kb=kb: (kb, j)))
        else:
            in_specs.append(pl.BlockSpec((None, kw, tn), lambda i, j, kb=kb: (layer, kb, j)))
        args.append(arr)
    if residual is not None:
        in_specs.append(pl.BlockSpec((tm, tn), lambda i, j: (i, j)))
        args.append(residual)
    return pl.pallas_call(
        functools.partial(_matmul_sum_kernel, n_ops=len(a_ops), has_res=residual is not None),
        grid=(m // tm, n_out // tn),
        in_specs=in_specs,
        out_specs=pl.BlockSpec((tm, tn), lambda i, j: (i, j)),
        out_shape=jax.ShapeDtypeStruct((m, n_out), out_dtype),
        compiler_params=_params(("parallel", "parallel"), vmem_mib),
        name="matmul_sum",
    )(*args)


def _swiglu_kernel(x_ref, wg_ref, wu_ref, o_ref):
    x = x_ref[...]
    gate = _dot(x, wg_ref[...])
    up = _dot(x, wu_ref[...])
    o_ref[...] = (gate * _sigmoid(gate) * up).astype(o_ref.dtype)


def swiglu_hidden(x, wg, wu, layer, tm, tn):
    m, k = x.shape
    f = wg.shape[2]
    tm, tn = _tile(m, tm), _tile(f, tn)
    w_spec = pl.BlockSpec((None, k, tn), lambda i, j: (layer, 0, j))
    return pl.pallas_call(
        _swiglu_kernel,
        grid=(m // tm, f // tn),
        in_specs=[pl.BlockSpec((tm, k), lambda i, j: (i, 0)), w_spec, w_spec],
        out_specs=pl.BlockSpec((tm, tn), lambda i, j: (i, j)),
        out_shape=jax.ShapeDtypeStruct((m, f), BF16),
        compiler_params=_params(("parallel", "parallel"), 48),
        name="swiglu_hidden",
    )(x, wg, wu)


def _banded_kernel(*refs, n_pairs, rep, max_dist, has_sink, want_lse):
    q_ref, kc_ref, kp_ref, vc_ref, vp_ref = refs[:5]
    pos = 5
    sink_ref = None
    if has_sink:
        sink_ref = refs[pos]
        pos += 1
    o_ref = refs[pos]
    lse_ref = refs[pos + 1] if want_lse else None

    blk = q_ref.shape[0]
    i = pl.program_id(1)
    row = lax.broadcasted_iota(jnp.int32, (blk, 2 * blk), 0)
    col = lax.broadcasted_iota(jnp.int32, (blk, 2 * blk), 1)
    dist = row + blk - col
    mask = (dist >= 0) & (dist <= max_dist) & ((col >= blk) | (i > 0))
    head0 = lax.broadcasted_iota(jnp.int32, (1, PAIR), 1) < HEAD_DIM

    for p in range(n_pairs):
        ql = pl.ds(p * PAIR, PAIR)
        kl = pl.ds((p // rep) * PAIR, PAIR)
        q = q_ref[:, ql]
        k = jnp.concatenate([kp_ref[:, kl], kc_ref[:, kl]], axis=0)
        v = jnp.concatenate([vp_ref[:, kl], vc_ref[:, kl]], axis=0)
        probs, inv_den, lse = [], [], []
        for e in range(2):
            sel = head0 if e == 0 else jnp.logical_not(head0)
            s = _dot_nt(jnp.where(sel, q, jnp.zeros_like(q)), k)
            s = jnp.where(mask, s, NEG_INF)
            mx = jnp.max(s, axis=-1, keepdims=True)
            if has_sink:
                sk = sink_ref[0:1, pl.ds(p * PAIR + e * HEAD_DIM, 1)]
                mx = jnp.maximum(mx, sk)
            pe = jnp.exp(s - mx)
            den = jnp.sum(pe, axis=-1, keepdims=True)
            if has_sink:
                den = den + jnp.exp(sk - mx)
            probs.append(pe.astype(BF16))
            inv_den.append(1.0 / den)
            lse.append(mx + jnp.log(den))
        v_stack = jnp.concatenate([jnp.where(head0, v, jnp.zeros_like(v)),
                                   jnp.where(head0, jnp.zeros_like(v), v)], axis=0)
        o = _dot(jnp.concatenate(probs, axis=1), v_stack)
        o = o * jnp.where(head0, inv_den[0], inv_den[1])
        o_ref[:, ql] = o.astype(o_ref.dtype)
        if want_lse:
            lse_ref[:, ql] = jnp.where(head0, lse[0], lse[1])


def banded_attention(q_arr, k_arr, v_arr, q_col, k_col, v_col, batch, seq, dil, n_pairs, n_kv_pairs,
                     max_dist, out_dtype, sink=None, want_lse=False):
    length = seq // dil
    nb = length // BLOCK
    rep = n_pairs // n_kv_pairs
    qw, kw = n_pairs * PAIR, n_kv_pairs * PAIR

    def view(arr):
        return arr.reshape(batch * length, dil * arr.shape[1])

    def spec(arr, width, c, shift):
        per_row = arr.shape[1] // width
        return pl.BlockSpec((BLOCK, width), lambda n, i: ((n // dil) * nb + jnp.maximum(i - shift, 0),
                                                          (n % dil) * per_row + c))

    in_specs = [spec(q_arr, qw, q_col, 0),
                spec(k_arr, kw, k_col, 0), spec(k_arr, kw, k_col, 1),
                spec(v_arr, kw, v_col, 0), spec(v_arr, kw, v_col, 1)]
    args = [view(q_arr), view(k_arr), view(k_arr), view(v_arr), view(v_arr)]
    if sink is not None:
        in_specs.append(pl.BlockSpec((1, qw), lambda n, i: (0, 0)))
        args.append(sink)
    out_spec = pl.BlockSpec((BLOCK, qw), lambda n, i: ((n // dil) * nb + i, n % dil))
    out_shape = [jax.ShapeDtypeStruct((batch * length, dil * qw), out_dtype)]
    out_specs = [out_spec]
    if want_lse:
        out_shape.append(jax.ShapeDtypeStruct((batch * length, dil * qw), F32))
        out_specs.append(out_spec)
    outs = pl.pallas_call(
        functools.partial(_banded_kernel, n_pairs=n_pairs, rep=rep, max_dist=max_dist,
                          has_sink=sink is not None, want_lse=want_lse),
        grid=(batch * dil, nb),
        in_specs=in_specs, out_specs=out_specs, out_shape=out_shape,
        compiler_params=_params(("parallel", "parallel"), 32),
        name="banded_attention",
    )(*args)
    return [o.reshape(batch * seq, qw) for o in outs]


def _dilated_merge_kernel(o1_ref, o2_ref, o3_ref, l1_ref, l2_ref, l3_ref, y_ref):
    l1, l2, l3 = l1_ref[...], l2_ref[...], l3_ref[...]
    mx = jnp.maximum(jnp.maximum(l1, l2), l3)
    e1, e2, e3 = jnp.exp(l1 - mx), jnp.exp(l2 - mx), jnp.exp(l3 - mx)
    num = e1 * o1_ref[...] + e2 * o2_ref[...] + e3 * o3_ref[...]
    y_ref[...] = (num / (e1 + e2 + e3)).astype(y_ref.dtype)


def dilated_merge(outs, lses, tm=256):
    m, w = outs[0].shape
    spec = pl.BlockSpec((tm, w), lambda i: (i, 0))
    return pl.pallas_call(
        _dilated_merge_kernel,
        grid=(m // tm,),
        in_specs=[spec] * 6, out_specs=spec,
        out_shape=jax.ShapeDtypeStruct((m, w), BF16),
        compiler_params=_params(("parallel",), 40),
        name="dilated_merge",
    )(*outs, *lses)


def _forget_cumsum_kernel(f_ref, b_ref, c_ref, carry_ref):
    @pl.when(pl.program_id(1) == 0)
    def _():
        carry_ref[...] = jnp.zeros_like(carry_ref)

    blk = f_ref.shape[0]
    log_f = -_softplus(-(f_ref[...] + b_ref[...]))
    tri = (lax.broadcasted_iota(jnp.int32, (blk, blk), 0) >= lax.broadcasted_iota(jnp.int32, (blk, blk), 1))
    c = _dot_exact_lhs(tri.astype(BF16), log_f) + carry_ref[...]
    c_ref[...] = c
    carry_ref[...] = c[blk - 1:blk, :]


def forget_cumsum(fc, b_f, batch, seq, blk=128):
    w = fc.shape[1]
    nb = seq // blk
    return pl.pallas_call(
        _forget_cumsum_kernel,
        grid=(batch, nb),
        in_specs=[pl.BlockSpec((blk, w), lambda b, i: (b * nb + i, 0)), pl.BlockSpec((1, w), lambda b, i: (0, 0))],
        out_specs=pl.BlockSpec((blk, w), lambda b, i: (b * nb + i, 0)),
        out_shape=jax.ShapeDtypeStruct(fc.shape, F32),
        scratch_shapes=[pltpu.VMEM((1, w), F32)],
        compiler_params=_params(("parallel", "arbitrary"), 16),
        name="forget_cumsum",
    )(fc, b_f)


FOX_ROWS = 32


def _fox_kernel(qi_ref, kj_ref, q_ref, k_ref, v_ref, cq_ref, ck_ref, o_ref,
                m_ref, l_ref, al_ref, cqr_ref, acc_ref, s_ref, p_ref):
    t = pl.program_id(2)
    i, j = qi_ref[t], kj_ref[t]
    tq, tk = q_ref.shape[0], k_ref.shape[0]
    n_tiles = tk // PAIR
    head0 = lax.broadcasted_iota(jnp.int32, (1, PAIR), 1) < HEAD_DIM

    @pl.when(j == 0)
    def _():
        m_ref[...] = jnp.full_like(m_ref, NEG_INF)
        l_ref[...] = jnp.zeros_like(l_ref)
        acc_ref[...] = jnp.zeros_like(acc_ref)
        for e in range(2):
            cqr_ref[e] = jnp.broadcast_to(cq_ref[0, e], (tq, PAIR))

    def step(masked):
        q, k, v = q_ref[...], k_ref[...], v_ref[...]
        lane = lax.broadcasted_iota(jnp.int32, (FOX_ROWS, PAIR), 1)
        sub = lax.broadcasted_iota(jnp.int32, (FOX_ROWS, PAIR), 0)
        for e in range(2):
            sel = head0 if e == 0 else jnp.logical_not(head0)
            s_ref[e] = _dot_nt(jnp.where(sel, q, jnp.zeros_like(q)), k)
        pv = []
        for e in range(2):
            sel = head0 if e == 0 else jnp.logical_not(head0)
            for c in range(tq // FOX_ROWS):
                rows = pl.ds(c * FOX_ROWS, FOX_ROWS)
                tiles = []
                for n in range(n_tiles):
                    cols = pl.ds(n * PAIR, PAIR)
                    s = s_ref[e, rows, cols] - ck_ref[0, e, :, cols]
                    if masked:
                        s = jnp.where(lane + n * PAIR <= sub + c * FOX_ROWS, s, NEG_INF)
                    tiles.append(s)
                mx = tiles[0]
                for s in tiles[1:]:
                    mx = jnp.maximum(mx, s)
                cq = cqr_ref[e, rows, :]
                m_prev = m_ref[e, rows, :]
                m_new = jnp.maximum(m_prev, jnp.max(mx, axis=-1, keepdims=True) + cq)
                shift = cq - m_new
                total = None
                for n, s in enumerate(tiles):
                    pe = jnp.exp(s + shift)
                    total = pe if total is None else total + pe
                    p_ref[rows, pl.ds(e * tk + n * PAIR, PAIR)] = pe.astype(BF16)
                alpha = jnp.exp(m_prev - m_new)
                l_ref[e, rows, :] = alpha * l_ref[e, rows, :] + jnp.sum(total, axis=-1, keepdims=True)
                m_ref[e, rows, :] = m_new
                al_ref[e, rows, :] = alpha
            pv.append(_dot(p_ref[:, pl.ds(e * tk, tk)], jnp.where(sel, v, jnp.zeros_like(v))))
        acc = acc_ref[...] * jnp.where(head0, al_ref[0], al_ref[1]) + pv[0] + pv[1]
        if masked:
            o_ref[...] = (acc / jnp.where(head0, l_ref[0], l_ref[1])).astype(o_ref.dtype)
        else:
            acc_ref[...] = acc

    @pl.when(j < i)
    def _():
        step(False)

    @pl.when(j == i)
    def _():
        step(True)


def fox_attention(qkv, c_col, c_row, batch, seq, n_pairs, q_col, k_col, v_col, tq):
    nq = seq // tq
    qi = jnp.asarray([i for i in range(nq) for _ in range(i + 1)], jnp.int32)
    kj = jnp.asarray([j for i in range(nq) for j in range(i + 1)], jnp.int32)
    rep = pltpu.VMEM((2, tq, PAIR), F32)
    grid_spec = pltpu.PrefetchScalarGridSpec(
        num_scalar_prefetch=2,
        grid=(batch, n_pairs, qi.shape[0]),
        in_specs=[pl.BlockSpec((tq, PAIR), lambda b, p, t, qi, kj: (b * nq + qi[t], q_col + p)),
                  pl.BlockSpec((tq, PAIR), lambda b, p, t, qi, kj: (b * nq + kj[t], k_col + p)),
                  pl.BlockSpec((tq, PAIR), lambda b, p, t, qi, kj: (b * nq + kj[t], v_col + p)),
                  pl.BlockSpec((1, 2, tq, 1), lambda b, p, t, qi, kj: (b, p, qi[t], 0)),
                  pl.BlockSpec((1, 2, 1, tq), lambda b, p, t, qi, kj: (b, p, 0, kj[t]))],
        out_specs=pl.BlockSpec((tq, PAIR), lambda b, p, t, qi, kj: (b * nq + qi[t], p)),
        scratch_shapes=[rep, rep, rep, rep, pltpu.VMEM((tq, PAIR), F32),
                        pltpu.VMEM((2, tq, tq), F32), pltpu.VMEM((tq, 2 * tq), BF16)])
    return pl.pallas_call(
        _fox_kernel,
        grid_spec=grid_spec,
        out_shape=jax.ShapeDtypeStruct((batch * seq, n_pairs * PAIR), BF16),
        compiler_params=_params(("parallel", "parallel", "arbitrary"), 32),
        name="fox_attention",
    )(qi, kj, qkv, qkv, qkv, c_col, c_row)


def _rwkv_prep_kernel(x_ref, xp_ref, mix_ref, w0_ref, w2_ref, a0_ref, a2_ref, g2_ref, kk_ref, ka_ref,
                      r_ref, lw_ref, k_ref, v_ref, a_ref, b_ref, g_ref, *, dim):
    first = pl.program_id(1) == 0
    x = x_ref[...]
    tt = x.shape[0]
    prev_row = jnp.where(first, 0.0, xp_ref[7:8, :])
    row = lax.broadcasted_iota(jnp.int32, (tt, 1), 0)
    shifted = jnp.where(row == 0, prev_row, pltpu.roll(x, 1, 0))
    x = x + (shifted - x) * mix_ref[...]

    r, k, v = x[:, 0:dim], x[:, dim:2 * dim], x[:, 2 * dim:3 * dim]
    o = 3 * dim
    xw, xa, xg = x[:, o:o + LORA_PAD], x[:, o + LORA_PAD:o + 2 * LORA_PAD], x[:, o + 2 * LORA_PAD:]
    w_log = -_softplus(-(w0_ref[...] + _dot_x3(jnp.tanh(xw), w2_ref[...]))) - 0.5
    a = _sigmoid(a0_ref[...] + _dot_x3(xa, a2_ref[...]))
    g = _dot_x3(_sigmoid(xg), g2_ref[...])

    kk = k * kk_ref[...]
    ones = _head_sum_matrix().astype(BF16)
    norm = []
    for p in range(dim // PAIR):
        kp = kk[:, p * PAIR:(p + 1) * PAIR]
        norm.append(jnp.sqrt(_dot_exact_rhs(kp * kp, ones)))
    kk = kk / jnp.maximum(jnp.concatenate(norm, axis=1), 1e-12)

    r_ref[...] = r
    lw_ref[...] = -jnp.exp(w_log)
    k_ref[...] = k * (1.0 + (a - 1.0) * ka_ref[...])
    v_ref[...] = v
    a_ref[...] = -kk
    b_ref[...] = kk * a
    g_ref[...] = g


def rwkv_prep(pa, mix, w0, w2, a0, a2, g2, k_k, k_a, batch, seq, dim, tt=128):
    m, cols = pa.shape
    nt = seq // tt
    full = lambda arr: pl.BlockSpec(arr.shape, lambda b, i: (0, 0))
    out_spec = pl.BlockSpec((tt, dim), lambda b, i: (b * nt + i, 0))
    consts = [mix, w0, w2, a0, a2, g2, k_k, k_a]
    return pl.pallas_call(
        functools.partial(_rwkv_prep_kernel, dim=dim),
        grid=(batch, nt),
        in_specs=[pl.BlockSpec((tt, cols), lambda b, i: (b * nt + i, 0)),
                  pl.BlockSpec((8, cols), lambda b, i: (jnp.maximum((b * nt + i) * (tt // 8) - 1, 0), 0))]
                 + [full(c) for c in consts],
        out_specs=[out_spec] * 7,
        out_shape=[jax.ShapeDtypeStruct((m, dim), F32)] * 7,
        compiler_params=_params(("parallel", "parallel"), 48),
        name="rwkv_prep",
    )(pa, pa, *consts)


def _stack_heads(x, head0):
    zero = jnp.zeros_like(x)
    return jnp.concatenate([jnp.where(head0, x, zero), jnp.where(head0, zero, x)], axis=0)


def _rwkv_scan_kernel(r_ref, lw_ref, k_ref, v_ref, a_ref, b_ref, g_ref, rk_ref, lnw_ref, lnb_ref,
                      o_ref, s_ref, y_ref):
    c_len = RWKV_CHUNK
    tt = r_ref.shape[0]
    chunks = range(tt // c_len)

    @pl.when(pl.program_id(2) == 0)
    def _():
        s_ref[...] = jnp.zeros_like(s_ref)

    head0 = lax.broadcasted_iota(jnp.int32, (1, PAIR), 1) < HEAD_DIM
    n2 = 2 * c_len
    row = lax.broadcasted_iota(jnp.int32, (n2, n2), 0)
    col = lax.broadcasted_iota(jnp.int32, (n2, n2), 1)
    t_loc, j_loc = row % c_len, col % c_len
    same_head = (row // c_len) == (col // c_len)
    strict = same_head & (j_loc < t_loc)
    incl = same_head & (j_loc <= t_loc)
    incl2 = jnp.concatenate([incl, incl], axis=1)
    sub_block = same_head & ((t_loc // 16) == (j_loc // 16))
    eye = (row == col).astype(F32)
    zeros = jnp.zeros((n2, PAIR), F32)
    trow = lax.broadcasted_iota(jnp.int32, (tt, tt), 0)
    tcol = lax.broadcasted_iota(jnp.int32, (tt, tt), 1)
    tri = ((trow >= tcol) & ((trow // c_len) == (tcol // c_len))).astype(BF16)

    r, lw, k, v, a, b = r_ref[...], lw_ref[...], k_ref[...], v_ref[...], a_ref[...], b_ref[...]
    cw = _dot_exact_lhs(tri, lw)
    e_pos, e_neg = jnp.exp(cw), jnp.exp(-cw)
    a_t, r_t, b_t, k_t = a * jnp.exp(cw - lw), r * e_pos, b * e_neg, k * e_neg

    def rows(x, c):
        return x[c * c_len:(c + 1) * c_len]

    e_last = [jnp.exp(rows(cw, c)[c_len - 1:c_len, :]) for c in chunks]
    a_s = [_stack_heads(rows(a_t, c), head0) for c in chunks]
    r_s = [_stack_heads(rows(r_t, c), head0) for c in chunks]
    v_s = [_stack_heads(rows(v, c), head0) for c in chunks]
    scores = [_dot_nt(jnp.concatenate([a_s[c], r_s[c]], axis=0).astype(BF16),
                      jnp.concatenate([_stack_heads(rows(b_t, c), head0),
                                       _stack_heads(rows(k_t, c), head0)], axis=0).astype(BF16)) for c in chunks]
    l_ab = [jnp.where(strict, s[:n2, :n2], 0.0) for s in scores]

    d1 = [jnp.where(sub_block, x, 0.0) for x in l_ab]
    d2 = [_dot_bf16(x, x) for x in d1]
    d4 = [_dot_bf16(x, x) for x in d2]
    d8 = [_dot_bf16(x, x) for x in d4]
    t_lo = [_dot_bf16(eye + x, eye + y) for x, y in zip(d1, d2)]
    t_hi = [_dot_bf16(eye + x, eye + y) for x, y in zip(d4, d8)]
    t_diag = [_dot_bf16(x, y) for x, y in zip(t_lo, t_hi)]
    n1 = [_dot_bf16(t, x - d) for t, x, d in zip(t_diag, l_ab, d1)]
    n_sq = [_dot_bf16(x, x) for x in n1]
    t_rest = [_dot_bf16(eye + x, eye + y) for x, y in zip(n1, n_sq)]
    t_inv = [_dot_bf16(x, t) for x, t in zip(t_rest, t_diag)]

    lv = [_dot_bf16(jnp.where(strict, s[:n2, n2:], 0.0), vs) for s, vs in zip(scores, v_s)]
    au = [_dot_bf16(t, jnp.concatenate([x, y], axis=1)) for t, x, y in zip(t_inv, a_s, lv)]
    z = []
    for c in chunks:
        e_end = e_last[c] * rows(e_neg, c)
        bend_s = _stack_heads(rows(b, c) * e_end, head0)
        kend_s = _stack_heads(rows(k, c) * e_end, head0)
        lhs = jnp.concatenate([jnp.where(incl2, scores[c][n2:], 0.0),
                               jnp.concatenate([bend_s.T, kend_s.T], axis=1)], axis=0)
        rhs = jnp.concatenate([au[c], jnp.concatenate([zeros, v_s[c]], axis=1)], axis=0)
        z.append(_dot_bf16(lhs, rhs))

    state = s_ref[...]
    for c in chunks:
        r_hat = r_s[c] + z[c][:n2, :PAIR]
        y_hat = z[c][:n2, PAIR:]
        lhs = jnp.concatenate([r_hat[:c_len] + r_hat[c_len:], z[c][n2:, :PAIR] + eye * e_last[c]], axis=0)
        prod = _dot_x3(lhs, state)
        y_ref[pl.ds(c * c_len, c_len), :] = prod[:c_len] + y_hat[:c_len] + y_hat[c_len:]
        state = prod[c_len:] + z[c][n2:, PAIR:]
    s_ref[...] = state

    ones = _head_sum_matrix().astype(BF16)
    y = y_ref[...]
    mu = _dot_exact_rhs(y, ones) * (1.0 / HEAD_DIM)
    d = y - mu
    var = _dot_exact_rhs(d * d, ones) * (1.0 / HEAD_DIM)
    yn = d * lax.rsqrt(var + RWKV_LN_EPS) * lnw_ref[...] + lnb_ref[...]
    bonus = _dot_exact_rhs(r * k * rk_ref[...], ones) * v
    o_ref[...] = ((yn + bonus) * g_ref[...]).astype(o_ref.dtype)


def rwkv_scan(r, lw, k, v, a, b, g, r_k, ln_w, ln_b, batch, seq, tt=512):
    m, dim = r.shape
    n_pairs = dim // PAIR
    nt = seq // tt
    seq_spec = pl.BlockSpec((tt, PAIR), lambda bb, p, i: (bb * nt + i, p))
    vec_spec = pl.BlockSpec((1, PAIR), lambda bb, p, i: (0, p))
    return pl.pallas_call(
        _rwkv_scan_kernel,
        grid=(batch, n_pairs, nt),
        in_specs=[seq_spec] * 7 + [vec_spec] * 3,
        out_specs=seq_spec,
        out_shape=jax.ShapeDtypeStruct((m, dim), BF16),
        scratch_shapes=[pltpu.VMEM((PAIR, PAIR), F32), pltpu.VMEM((tt, PAIR), F32)],
        compiler_params=_params(("parallel", "parallel", "arbitrary"), 32),
        name="rwkv_scan",
    )(r, lw, k, v, a, b, g, r_k, ln_w, ln_b)


def _pad_cols(w, width):
    return jnp.pad(w, ((0, 0), (0, width - w.shape[1])))


def _pad_rows(w, height):
    return jnp.pad(w, ((0, height - w.shape[0]), (0, 0)))


def _dup_kv_heads(w):
    d, c = w.shape
    return jnp.tile(w.reshape(d, c // HEAD_DIM, 1, HEAD_DIM), (1, 1, 2, 1)).reshape(d, 2 * c)


def even_mixer(h, xn, batch, seq, w_in, mix, w0, w2, a0, a2, g2, k_k, k_a, r_k, ln_w, ln_b, sink, w_out, layer,
               dim, swa_heads, swa_kv_heads, tm, tn):
    d_model = xn.shape[1]
    decay_lora, aaa_lora = w2.shape[0], a2.shape[0]
    o = 3 * dim
    o_a, o_g = o + decay_lora, o + decay_lora + aaa_lora
    rwkv_cols = o_g + g2.shape[0]
    scale = HEAD_DIM ** -0.5

    def rwkv_layout(t):
        return jnp.concatenate([t[..., :o], _pad_cols(t[..., o:o_a], LORA_PAD), _pad_cols(t[..., o_a:o_g], LORA_PAD),
                                t[..., o_g:rwkv_cols]], axis=-1)

    w_a = rwkv_layout(w_in).astype(BF16)
    q_end = rwkv_cols + swa_heads * HEAD_DIM
    k_end = q_end + swa_kv_heads * HEAD_DIM
    w_b = jnp.concatenate([w_in[:, rwkv_cols:q_end] * scale, _dup_kv_heads(w_in[:, q_end:k_end]),
                           _dup_kv_heads(w_in[:, k_end:])], axis=1).astype(BF16)

    pa = matmul_sum([(xn, d_model, 0)], [(w_a, d_model, 0)], w_a.shape[1], F32, tm, tn)
    pb = matmul_sum([(xn, d_model, 0)], [(w_b, d_model, 0)], w_b.shape[1], BF16, tm, tn)

    row = lambda t: t.reshape(1, -1)
    prep = rwkv_prep(pa, rwkv_layout(row(mix)), row(w0), _pad_rows(w2, LORA_PAD), row(a0), _pad_rows(a2, LORA_PAD),
                     g2, row(k_k), row(k_a), batch, seq, dim)
    y_a = rwkv_scan(*prep, row(r_k), row(ln_w), row(ln_b), batch, seq)

    q_pairs, kv_pairs = swa_heads // 2, swa_kv_heads
    kv_w = kv_pairs * PAIR
    kv_base = (q_pairs * PAIR) // kv_w
    sink_lanes = jnp.repeat(sink, HEAD_DIM).reshape(1, -1)
    (y_b,) = banded_attention(pb, pb, pb, 0, kv_base, kv_base + 1, batch, seq, 1, q_pairs, kv_pairs,
                              SWA_WINDOW - 1, BF16, sink=sink_lanes)

    return _out_proj(h, y_a, y_b, w_out, layer, tm, tn)


def odd_mixer(h, xn, batch, seq, w_in, b_f, w_out, layer, fox_heads, dil_heads, tm, tn, fox_tq):
    d_model = xn.shape[1]
    cw, dw = fox_heads * HEAD_DIM, dil_heads * HEAD_DIM
    scale = HEAD_DIM ** -0.5
    f0 = 3 * cw
    d0 = f0 + fox_heads
    w_main = jnp.concatenate([w_in[:, :cw] * scale, w_in[:, cw:f0], w_in[:, d0:d0 + dw] * scale,
                              w_in[:, d0 + dw:]], axis=1).astype(BF16)
    w_f = _pad_cols(w_in[:, f0:d0], PAIR).astype(BF16)
    proj = matmul_sum([(xn, d_model, 0)], [(w_main, d_model, 0)], w_main.shape[1], BF16, tm, tn)
    fc = matmul_sum([(xn, d_model, 0)], [(w_f, d_model, 0)], PAIR, F32, tm, PAIR)

    c = forget_cumsum(fc, _pad_cols(b_f.reshape(1, -1), PAIR), batch, seq)
    c = c[:, :fox_heads].reshape(batch, seq, fox_heads).transpose(0, 2, 1)
    fox_pairs = fox_heads // 2
    y_c = fox_attention(proj, c[..., None], c[:, :, None, :], batch, seq, fox_pairs, 0, fox_pairs, 2 * fox_pairs,
                        fox_tq)

    dil_pairs = dil_heads // 2
    base = f0 // dw
    outs, lses = [], []
    for window, dil in DILATED_PAIRS:
        o, lse = banded_attention(proj, proj, proj, base, base + 1, base + 2, batch, seq, dil, dil_pairs, dil_pairs,
                                  window // dil, F32, want_lse=True)
        outs.append(o)
        lses.append(lse)
    y_d = dilated_merge(outs, lses)

    return _out_proj(h, y_c, y_d, w_out, layer, tm, tn)


def _out_proj(h, y_1, y_2, w_out, layer, tm, tn):
    w1, w2 = y_1.shape[1], y_2.shape[1]
    kw = math.gcd(w1, w2)
    n1, n2 = w1 // kw, w2 // kw
    a_ops = [(y_1, kw, c) for c in range(n1)] + [(y_2, kw, c) for c in range(n2)]
    b_ops = [(w_out, kw, c) for c in range(n1 + n2)]
    return matmul_sum(a_ops, b_ops, w_out.shape[2], F32, tm, tn, residual=h, layer=layer)


def ffn(h, xn, w_gate, w_up, w_down, layer, tm, tn_hidden, tm_down, tn_down):
    half = w_gate.shape[2] // 2
    hidden = swiglu_hidden(xn, w_gate, w_up, layer, tm, tn_hidden)
    return matmul_sum([(hidden, half, 0), (hidden, half, 1)], [(w_down, half, 0), (w_down, half, 1)],
                      w_down.shape[2], F32, tm_down, tn_down, residual=h, layer=layer, vmem_mib=56)


def kernel(x, norm_mix, norm_ffn, norm_final, ffn_w_gate, ffn_w_up, ffn_w_down, ev_w_in, rwkv_mix, rwkv_w0, rwkv_w2, rwkv_a0, rwkv_a2, rwkv_g2, rwkv_k_k, rwkv_k_a, rwkv_r_k, rwkv_ln_w, rwkv_ln_b, swa_sink, ev_w_out, od_w_in, fox_b_f, od_w_out):
    batch, seq, d_model = x.shape
    depth = norm_mix.shape[0]
    dim = rwkv_w0.shape[1]
    swa_heads = swa_sink.shape[1]
    fox_heads = fox_b_f.shape[1]
    dil_heads = d_model // HEAD_DIM - fox_heads
    tm, tn = 1024, 512
    w_gate, w_up, w_down = ffn_w_gate.astype(BF16), ffn_w_up.astype(BF16), ffn_w_down.astype(BF16)
    w_out_even, w_out_odd = ev_w_out.astype(BF16), od_w_out.astype(BF16)
    h = x.reshape(batch * seq, d_model)
    for layer in range(depth):
        i = layer // 2
        xn = rmsnorm(h, norm_mix[layer], BF16)
        if layer % 2 == 0:
            h = even_mixer(h, xn, batch, seq, ev_w_in[i], rwkv_mix[i], rwkv_w0[i], rwkv_w2[i], rwkv_a0[i], rwkv_a2[i],
                           rwkv_g2[i], rwkv_k_k[i], rwkv_k_a[i], rwkv_r_k[i].reshape(-1), rwkv_ln_w[i], rwkv_ln_b[i],
                           swa_sink[i], w_out_even, i, dim, swa_heads, swa_heads // 8, tm, tn)
        else:
            h = odd_mixer(h, xn, batch, seq, od_w_in[i], fox_b_f[i], w_out_odd, i, fox_heads, dil_heads, tm, tn, 512)
        xn = rmsnorm(h, norm_ffn[layer], BF16)
        h = ffn(h, xn, w_gate, w_up, w_down, layer, tm, 256, 512, 256)
    return rmsnorm(h, norm_final, F32).reshape(batch, seq, d_model)
```

```python
import functools
import math

import jax
import jax.numpy as jnp
from jax import lax
from jax.experimental import pallas as pl
from jax.experimental.pallas import tpu as pltpu

F32 = jnp.float32
BF16 = jnp.bfloat16
HIGHEST = lax.Precision.HIGHEST

HEAD_DIM = 64
PAIR = 2 * HEAD_DIM
BLOCK = 128
SWA_WINDOW = 128
DILATED_PAIRS = ((128, 1), (512, 4), (2048, 16))
RMS_EPS = 1e-6
RWKV_LN_EPS = 64e-5
NEG_INF = -1e30
DILATED_GROUP = 4
RWKV_CHUNK = 64
LORA_PAD = 128
MIB = 1024 * 1024


def _params(semantics, vmem_mib):
    return pltpu.CompilerParams(dimension_semantics=semantics, vmem_limit_bytes=vmem_mib * MIB)


def _dot(a, b, precision=None):
    return jnp.dot(a, b, preferred_element_type=F32, precision=precision)


def _dot_bf16(a, b):
    return jnp.dot(a.astype(BF16), b.astype(BF16), preferred_element_type=F32)


def _dot_nt(a, b, precision=None):
    return lax.dot_general(a, b, (((1,), (1,)), ((), ())), preferred_element_type=F32, precision=precision)


def _split3(x):
    hi = x.astype(BF16)
    rem = x - hi.astype(F32)
    mid = rem.astype(BF16)
    lo = (rem - mid.astype(F32)).astype(BF16)
    return hi, mid, lo


def _dot_exact_lhs(m, x):
    hi, mid, lo = _split3(x)
    return _dot(m, lo) + _dot(m, mid) + _dot(m, hi)


def _dot_exact_rhs(x, m):
    hi, mid, lo = _split3(x)
    return _dot(lo, m) + _dot(mid, m) + _dot(hi, m)


def _dot_x3(a, b):
    a_hi, b_hi = a.astype(BF16), b.astype(BF16)
    a_lo, b_lo = (a - a_hi.astype(F32)).astype(BF16), (b - b_hi.astype(F32)).astype(BF16)
    return _dot(a_lo, b_hi) + _dot(a_hi, b_lo) + _dot(a_hi, b_hi)


def _sigmoid(x):
    return 1.0 / (1.0 + jnp.exp(-x))


def _softplus(x):
    return jnp.maximum(x, 0.0) + jnp.log(1.0 + jnp.exp(-jnp.abs(x)))


def _tile(n, target):
    t = min(n, target) // PAIR * PAIR
    while n % t:
        t -= PAIR
    return t


def _head_sum_matrix():
    r = lax.broadcasted_iota(jnp.int32, (PAIR, PAIR), 0) // HEAD_DIM
    c = lax.broadcasted_iota(jnp.int32, (PAIR, PAIR), 1) // HEAD_DIM
    return (r == c).astype(F32)


def _rmsnorm_kernel(x_ref, g_ref, o_ref):
    x = x_ref[...]
    ms = jnp.mean(x * x, axis=-1, keepdims=True)
    o_ref[...] = (x * lax.rsqrt(ms + RMS_EPS) * g_ref[...]).astype(o_ref.dtype)


def rmsnorm(x, g, out_dtype, tm=256):
    m, d = x.shape
    return pl.pallas_call(
        _rmsnorm_kernel,
        grid=(m // tm,),
        in_specs=[pl.BlockSpec((tm, d), lambda i: (i, 0)), pl.BlockSpec((1, d), lambda i: (0, 0))],
        out_specs=pl.BlockSpec((tm, d), lambda i: (i, 0)),
        out_shape=jax.ShapeDtypeStruct((m, d), out_dtype),
        compiler_params=_params(("parallel",), 40),
        name="rmsnorm",
    )(x, g.reshape(1, d))


def _matmul_sum_kernel(*refs, n_ops, has_res):
    a_refs, b_refs = refs[:n_ops], refs[n_ops:2 * n_ops]
    o_ref = refs[-1]
    acc = _dot(a_refs[0][...], b_refs[0][...])
    for a_ref, b_ref in zip(a_refs[1:], b_refs[1:]):
        acc = acc + _dot(a_ref[...], b_ref[...])
    if has_res:
        acc = acc + refs[2 * n_ops][...]
    o_ref[...] = acc.astype(o_ref.dtype)


def matmul_sum(a_ops, b_ops, n_out, out_dtype, tm, tn, residual=None, layer=None, vmem_mib=48):
    m = a_ops[0][0].shape[0]
    tm, tn = _tile(m, tm), _tile(n_out, tn)
    in_specs, args = [], []
    for arr, kw, kb in a_ops:
        in_specs.append(pl.BlockSpec((tm, kw), lambda i, j, kb=kb: (i, kb)))
        args.append(arr)
    for arr, kw, kb in b_ops:
        if layer is None:
            in_specs.append(pl.BlockSpec((kw, tn), lambda i, j, kb=kb: (kb, j)))
        else:
            in_specs.append(pl.BlockSpec((None, kw, tn), lambda i, j, kb=kb: (layer, kb, j)))
        args.append(arr)
    if residual is not None:
        in_specs.append(pl.BlockSpec((tm, tn), lambda i, j: (i, j)))
        args.append(residual)
    return pl.pallas_call(
        functools.partial(_matmul_sum_kernel, n_ops=len(a_ops), has_res=residual is not None),
        grid=(m // tm, n_out // tn),
        in_specs=in_specs,
        out_specs=pl.BlockSpec((tm, tn), lambda i, j: (i, j)),
        out_shape=jax.ShapeDtypeStruct((m, n_out), out_dtype),
        compiler_params=_params(("parallel", "parallel"), vmem_mib),
        name="matmul_sum",
    )(*args)


def _swiglu_kernel(x_ref, wg_ref, wu_ref, o_ref):
    x = x_ref[...]
    gate = _dot(x, wg_ref[...])
    up = _dot(x, wu_ref[...])
    o_ref[...] = (gate * _sigmoid(gate) * up).astype(o_ref.dtype)


def swiglu_hidden(x, wg, wu, layer, tm, tn):
    m, k = x.shape
    f = wg.shape[2]
    tm, tn = _tile(m, tm), _tile(f, tn)
    w_spec = pl.BlockSpec((None, k, tn), lambda i, j: (layer, 0, j))
    return pl.pallas_call(
        _swiglu_kernel,
        grid=(m // tm, f // tn),
        in_specs=[pl.BlockSpec((tm, k), lambda i, j: (i, 0)), w_spec, w_spec],
        out_specs=pl.BlockSpec((tm, tn), lambda i, j: (i, j)),
        out_shape=jax.ShapeDtypeStruct((m, f), BF16),
        compiler_params=_params(("parallel", "parallel"), 48),
        name="swiglu_hidden",
    )(x, wg, wu)


def _band_mask(max_dist, has_prev):
    row = lax.broadcasted_iota(jnp.int32, (BLOCK, 2 * BLOCK), 0)
    col = lax.broadcasted_iota(jnp.int32, (BLOCK, 2 * BLOCK), 1)
    dist = row + BLOCK - col
    return (dist >= 0) & (dist <= max_dist) & ((col >= BLOCK) | has_prev)


def _pair_attention_group(qs, ks, vs, mask, head0, sinks=None):
    g = len(qs)
    masked_q = [jnp.where(head0 if e == 0 else jnp.logical_not(head0), q, jnp.zeros_like(q))
                for q in qs for e in range(2)]
    if len(ks) == 1:
        s = _dot_nt(jnp.concatenate(masked_q, axis=0), ks[0]).reshape(2 * g, BLOCK, 2 * BLOCK)
    else:
        s = jnp.stack([_dot_nt(q, ks[u // 2]) for u, q in enumerate(masked_q)])
    s = jnp.where(mask[None], s, NEG_INF)
    mx = jnp.max(s, axis=-1, keepdims=True)
    if sinks is not None:
        sk = jnp.stack(sinks)
        mx = jnp.maximum(mx, sk)
    pe = jnp.exp(s - mx)
    den = jnp.sum(pe, axis=-1, keepdims=True)
    if sinks is not None:
        den = den + jnp.exp(sk - mx)
    probs = pe.astype(BF16)
    inv_den = 1.0 / den
    lse = mx + jnp.log(den)

    def stack_heads(v):
        return jnp.concatenate([jnp.where(head0, v, jnp.zeros_like(v)), jnp.where(head0, jnp.zeros_like(v), v)], axis=0)

    p_cat = [jnp.concatenate([probs[2 * u], probs[2 * u + 1]], axis=1) for u in range(g)]
    if len(vs) == 1:
        o_all = _dot(jnp.concatenate(p_cat, axis=0), stack_heads(vs[0]))
        outs = [o_all[u * BLOCK:(u + 1) * BLOCK] for u in range(g)]
    else:
        outs = [_dot(p, stack_heads(v)) for p, v in zip(p_cat, vs)]
    return [(o * jnp.where(head0, inv_den[2 * u], inv_den[2 * u + 1]), jnp.where(head0, lse[2 * u], lse[2 * u + 1]))
            for u, o in enumerate(outs)]


def _swa_kernel(q_ref, kc_ref, kp_ref, vc_ref, vp_ref, sink_ref, o_ref, *, n_pairs, rep, max_dist):
    mask = _band_mask(max_dist, pl.program_id(1) > 0)
    head0 = lax.broadcasted_iota(jnp.int32, (1, PAIR), 1) < HEAD_DIM
    for kv in range(n_pairs // rep):
        kl = pl.ds(kv * PAIR, PAIR)
        k = jnp.concatenate([kp_ref[:, kl], kc_ref[:, kl]], axis=0)
        v = jnp.concatenate([vp_ref[:, kl], vc_ref[:, kl]], axis=0)
        pairs = range(kv * rep, (kv + 1) * rep)
        sinks = [sink_ref[0:1, pl.ds(p * PAIR + e * HEAD_DIM, 1)] for p in pairs for e in range(2)]
        outs = _pair_attention_group([q_ref[:, pl.ds(p * PAIR, PAIR)] for p in pairs], [k], [v], mask, head0, sinks)
        for p, (o, _) in zip(pairs, outs):
            o_ref[:, pl.ds(p * PAIR, PAIR)] = o.astype(o_ref.dtype)


def swa_attention(qkv, q_col, k_col, v_col, batch, seq, n_pairs, n_kv_pairs, max_dist, sink):
    nb = seq // BLOCK
    qw, kw = n_pairs * PAIR, n_kv_pairs * PAIR
    cur = lambda c: (lambda b, i: (b * nb + i, c))
    prev = lambda c: (lambda b, i: (b * nb + jnp.maximum(i - 1, 0), c))
    return pl.pallas_call(
        functools.partial(_swa_kernel, n_pairs=n_pairs, rep=n_pairs // n_kv_pairs, max_dist=max_dist),
        grid=(batch, nb),
        in_specs=[pl.BlockSpec((BLOCK, qw), cur(q_col)),
                  pl.BlockSpec((BLOCK, kw), cur(k_col)), pl.BlockSpec((BLOCK, kw), prev(k_col)),
                  pl.BlockSpec((BLOCK, kw), cur(v_col)), pl.BlockSpec((BLOCK, kw), prev(v_col)),
                  pl.BlockSpec((1, qw), lambda b, i: (0, 0))],
        out_specs=pl.BlockSpec((BLOCK, qw), cur(0)),
        out_shape=jax.ShapeDtypeStruct((batch * seq, qw), BF16),
        compiler_params=_params(("parallel", "parallel"), 32),
        name="swa_attention",
    )(qkv, qkv, qkv, qkv, qkv, sink)


def _dilated_kernel(q_ref, kc_ref, kp_ref, vc_ref, vp_ref, y_ref, o_ref, lse_ref):
    slab = q_ref.shape[0]
    not_first = pl.program_id(1) > 0
    head0 = lax.broadcasted_iota(jnp.int32, (1, PAIR), 1) < HEAD_DIM
    for branch, (window, dil) in enumerate(DILATED_PAIRS):
        span = BLOCK * dil
        inner_mask = _band_mask(window // dil, True)
        edge_mask = _band_mask(window // dil, not_first)
        units = [(blk, p) for blk in range(slab // span) for p in range(dil)]
        edge_units = [u for u in units if u[0] == 0]
        inner_units = [u for u in units if u[0] > 0]
        groups = [(edge_units[g:g + DILATED_GROUP], edge_mask) for g in range(0, len(edge_units), DILATED_GROUP)]
        groups += [(inner_units[g:g + DILATED_GROUP], inner_mask) for g in range(0, len(inner_units), DILATED_GROUP)]
        for group, mask in groups:
            qs, ks, vs, rows = [], [], [], []
            for blk, p in group:
                cur = pl.ds(blk * span + p, BLOCK, stride=dil)
                if blk:
                    before = pl.ds((blk - 1) * span + p, BLOCK, stride=dil)
                    k_prev, v_prev = kc_ref[before, :], vc_ref[before, :]
                else:
                    before = pl.ds(slab - span + p, BLOCK, stride=dil)
                    k_prev, v_prev = kp_ref[before, :], vp_ref[before, :]
                qs.append(q_ref[cur, :].astype(BF16))
                ks.append(jnp.concatenate([k_prev, kc_ref[cur, :]], axis=0).astype(BF16))
                vs.append(jnp.concatenate([v_prev, vc_ref[cur, :]], axis=0).astype(BF16))
                rows.append(cur)
            for cur, (o, lse) in zip(rows, _pair_attention_group(qs, ks, vs, mask, head0)):
                o_ref[branch, cur, :] = o
                lse_ref[branch, cur, :] = lse
    l1, l2, l3 = lse_ref[0], lse_ref[1], lse_ref[2]
    mx = jnp.maximum(jnp.maximum(l1, l2), l3)
    e1, e2, e3 = jnp.exp(l1 - mx), jnp.exp(l2 - mx), jnp.exp(l3 - mx)
    num = e1 * o_ref[0] + e2 * o_ref[1] + e3 * o_ref[2]
    y_ref[...] = (num / (e1 + e2 + e3)).astype(y_ref.dtype)


def dilated_attention(qkv, batch, seq, n_pairs):
    slab = BLOCK * max(dil for _, dil in DILATED_PAIRS)
    ns = seq // slab
    cur = lambda base: (lambda b, s, p: (b * ns + s, base + p))
    prev = lambda base: (lambda b, s, p: (b * ns + jnp.maximum(s - 1, 0), base + p))
    block = lambda index_map: pl.BlockSpec((slab, PAIR), index_map)
    n_branches = len(DILATED_PAIRS)
    return pl.pallas_call(
        _dilated_kernel,
        grid=(batch, ns, n_pairs),
        in_specs=[block(cur(0)), block(cur(n_pairs)), block(prev(n_pairs)),
                  block(cur(2 * n_pairs)), block(prev(2 * n_pairs))],
        out_specs=block(cur(0)),
        out_shape=jax.ShapeDtypeStruct((batch * seq, n_pairs * PAIR), BF16),
        scratch_shapes=[pltpu.VMEM((n_branches, slab, PAIR), F32), pltpu.VMEM((n_branches, slab, PAIR), F32)],
        compiler_params=_params(("parallel", "parallel", "parallel"), 40),
        name="dilated_attention",
    )(qkv, qkv, qkv, qkv, qkv)


def _forget_cumsum_kernel(f_ref, b_ref, c_ref, carry_ref):
    @pl.when(pl.program_id(1) == 0)
    def _():
        carry_ref[...] = jnp.zeros_like(carry_ref)

    blk = f_ref.shape[0]
    log_f = -_softplus(-(f_ref[...] + b_ref[...]))
    tri = (lax.broadcasted_iota(jnp.int32, (blk, blk), 0) >= lax.broadcasted_iota(jnp.int32, (blk, blk), 1))
    c = _dot_exact_lhs(tri.astype(BF16), log_f) + carry_ref[...]
    c_ref[...] = c
    carry_ref[...] = c[blk - 1:blk, :]


def forget_cumsum(fc, b_f, batch, seq, blk=128):
    w = fc.shape[1]
    nb = seq // blk
    return pl.pallas_call(
        _forget_cumsum_kernel,
        grid=(batch, nb),
        in_specs=[pl.BlockSpec((blk, w), lambda b, i: (b * nb + i, 0)), pl.BlockSpec((1, w), lambda b, i: (0, 0))],
        out_specs=pl.BlockSpec((blk, w), lambda b, i: (b * nb + i, 0)),
        out_shape=jax.ShapeDtypeStruct(fc.shape, F32),
        scratch_shapes=[pltpu.VMEM((1, w), F32)],
        compiler_params=_params(("parallel", "arbitrary"), 16),
        name="forget_cumsum",
    )(fc, b_f)


FOX_ROWS = 32


def _fox_kernel(qi_ref, kj_ref, q_ref, k_ref, v_ref, cq_ref, ck_ref, o_ref,
                m_ref, l_ref, al_ref, cqr_ref, acc_ref, s_ref, p_ref):
    t = pl.program_id(2)
    i, j = qi_ref[t], kj_ref[t]
    tq, tk = q_ref.shape[0], k_ref.shape[0]
    n_tiles = tk // PAIR
    head0 = lax.broadcasted_iota(jnp.int32, (1, PAIR), 1) < HEAD_DIM

    @pl.when(j == 0)
    def _():
        m_ref[...] = jnp.full_like(m_ref, NEG_INF)
        l_ref[...] = jnp.zeros_like(l_ref)
        acc_ref[...] = jnp.zeros_like(acc_ref)
        for e in range(2):
            cqr_ref[e] = jnp.broadcast_to(cq_ref[0, e], (tq, PAIR))

    def step(masked):
        q, k, v = q_ref[...], k_ref[...], v_ref[...]
        lane = lax.broadcasted_iota(jnp.int32, (FOX_ROWS, PAIR), 1)
        sub = lax.broadcasted_iota(jnp.int32, (FOX_ROWS, PAIR), 0)
        for e in range(2):
            sel = head0 if e == 0 else jnp.logical_not(head0)
            s_ref[e] = _dot_nt(jnp.where(sel, q, jnp.zeros_like(q)), k)
        pv = []
        for e in range(2):
            sel = head0 if e == 0 else jnp.logical_not(head0)
            for c in range(tq // FOX_ROWS):
                rows = pl.ds(c * FOX_ROWS, FOX_ROWS)
                tiles = []
                for n in range(n_tiles):
                    cols = pl.ds(n * PAIR, PAIR)
                    s = s_ref[e, rows, cols] - ck_ref[0, e, :, cols]
                    if masked:
                        s = jnp.where(lane + n * PAIR <= sub + c * FOX_ROWS, s, NEG_INF)
                    tiles.append(s)
                mx = tiles[0]
                for s in tiles[1:]:
                    mx = jnp.maximum(mx, s)
                cq = cqr_ref[e, rows, :]
                m_prev = m_ref[e, rows, :]
                m_new = jnp.maximum(m_prev, jnp.max(mx, axis=-1, keepdims=True) + cq)
                shift = cq - m_new
                total = None
                for n, s in enumerate(tiles):
                    pe = jnp.exp(s + shift)
                    total = pe if total is None else total + pe
                    p_ref[rows, pl.ds(e * tk + n * PAIR, PAIR)] = pe.astype(BF16)
                alpha = jnp.exp(m_prev - m_new)
                l_ref[e, rows, :] = alpha * l_ref[e, rows, :] + jnp.sum(total, axis=-1, keepdims=True)
                m_ref[e, rows, :] = m_new
                al_ref[e, rows, :] = alpha
            pv.append(_dot(p_ref[:, pl.ds(e * tk, tk)], jnp.where(sel, v, jnp.zeros_like(v))))
        acc = acc_ref[...] * jnp.where(head0, al_ref[0], al_ref[1]) + pv[0] + pv[1]
        if masked:
            o_ref[...] = (acc / jnp.where(head0, l_ref[0], l_ref[1])).astype(o_ref.dtype)
        else:
            acc_ref[...] = acc

    @pl.when(j < i)
    def _():
        step(False)

    @pl.when(j == i)
    def _():
        step(True)


def fox_attention(qkv, c_col, c_row, batch, seq, n_pairs, q_col, k_col, v_col, tq):
    nq = seq // tq
    qi = jnp.asarray([i for i in range(nq) for _ in range(i + 1)], jnp.int32)
    kj = jnp.asarray([j for i in range(nq) for j in range(i + 1)], jnp.int32)
    rep = pltpu.VMEM((2, tq, PAIR), F32)
    grid_spec = pltpu.PrefetchScalarGridSpec(
        num_scalar_prefetch=2,
        grid=(batch, n_pairs, qi.shape[0]),
        in_specs=[pl.BlockSpec((tq, PAIR), lambda b, p, t, qi, kj: (b * nq + qi[t], q_col + p)),
                  pl.BlockSpec((tq, PAIR), lambda b, p, t, qi, kj: (b * nq + kj[t], k_col + p)),
                  pl.BlockSpec((tq, PAIR), lambda b, p, t, qi, kj: (b * nq + kj[t], v_col + p)),
                  pl.BlockSpec((1, 2, tq, 1), lambda b, p, t, qi, kj: (b, p, qi[t], 0)),
                  pl.BlockSpec((1, 2, 1, tq), lambda b, p, t, qi, kj: (b, p, 0, kj[t]))],
        out_specs=pl.BlockSpec((tq, PAIR), lambda b, p, t, qi, kj: (b * nq + qi[t], p)),
        scratch_shapes=[rep, rep, rep, rep, pltpu.VMEM((tq, PAIR), F32),
                        pltpu.VMEM((2, tq, tq), F32), pltpu.VMEM((tq, 2 * tq), BF16)])
    return pl.pallas_call(
        _fox_kernel,
        grid_spec=grid_spec,
        out_shape=jax.ShapeDtypeStruct((batch * seq, n_pairs * PAIR), BF16),
        compiler_params=_params(("parallel", "parallel", "arbitrary"), 32),
        name="fox_attention",
    )(qi, kj, qkv, qkv, qkv, c_col, c_row)


def _rwkv_prep_kernel(x_ref, xp_ref, mix_ref, w0_ref, w2_ref, a0_ref, a2_ref, g2_ref, kk_ref, ka_ref,
                      r_ref, lw_ref, k_ref, v_ref, a_ref, b_ref, g_ref, *, dim):
    first = pl.program_id(1) == 0
    x = x_ref[...]
    tt = x.shape[0]
    prev_row = jnp.where(first, 0.0, xp_ref[7:8, :])
    row = lax.broadcasted_iota(jnp.int32, (tt, 1), 0)
    shifted = jnp.where(row == 0, prev_row, pltpu.roll(x, 1, 0))
    x = x + (shifted - x) * mix_ref[...]

    r, k, v = x[:, 0:dim], x[:, dim:2 * dim], x[:, 2 * dim:3 * dim]
    o = 3 * dim
    xw, xa, xg = x[:, o:o + LORA_PAD], x[:, o + LORA_PAD:o + 2 * LORA_PAD], x[:, o + 2 * LORA_PAD:]
    w_log = -_softplus(-(w0_ref[...] + _dot_x3(jnp.tanh(xw), w2_ref[...]))) - 0.5
    a = _sigmoid(a0_ref[...] + _dot_x3(xa, a2_ref[...]))
    g = _dot_x3(_sigmoid(xg), g2_ref[...])

    kk = k * kk_ref[...]
    ones = _head_sum_matrix().astype(BF16)
    norm = []
    for p in range(dim // PAIR):
        kp = kk[:, p * PAIR:(p + 1) * PAIR]
        norm.append(jnp.sqrt(_dot_exact_rhs(kp * kp, ones)))
    kk = kk / jnp.maximum(jnp.concatenate(norm, axis=1), 1e-12)

    r_ref[...] = r
    lw_ref[...] = -jnp.exp(w_log)
    k_ref[...] = k * (1.0 + (a - 1.0) * ka_ref[...])
    v_ref[...] = v
    a_ref[...] = -kk
    b_ref[...] = kk * a
    g_ref[...] = g


def rwkv_prep(pa, mix, w0, w2, a0, a2, g2, k_k, k_a, batch, seq, dim, tt=128):
    m, cols = pa.shape
    nt = seq // tt
    full = lambda arr: pl.BlockSpec(arr.shape, lambda b, i: (0, 0))
    out_spec = pl.BlockSpec((tt, dim), lambda b, i: (b * nt + i, 0))
    consts = [mix, w0, w2, a0, a2, g2, k_k, k_a]
    return pl.pallas_call(
        functools.partial(_rwkv_prep_kernel, dim=dim),
        grid=(batch, nt),
        in_specs=[pl.BlockSpec((tt, cols), lambda b, i: (b * nt + i, 0)),
                  pl.BlockSpec((8, cols), lambda b, i: (jnp.maximum((b * nt + i) * (tt // 8) - 1, 0), 0))]
                 + [full(c) for c in consts],
        out_specs=[out_spec] * 7,
        out_shape=[jax.ShapeDtypeStruct((m, dim), F32)] * 7,
        compiler_params=_params(("parallel", "parallel"), 48),
        name="rwkv_prep",
    )(pa, pa, *consts)


def _stack_heads(x, head0):
    zero = jnp.zeros_like(x)
    return jnp.concatenate([jnp.where(head0, x, zero), jnp.where(head0, zero, x)], axis=0)


def _rwkv_scan_kernel(r_ref, lw_ref, k_ref, v_ref, a_ref, b_ref, g_ref, rk_ref, lnw_ref, lnb_ref,
                      o_ref, s_ref, y_ref):
    c_len = RWKV_CHUNK
    tt = r_ref.shape[0]
    chunks = range(tt // c_len)

    @pl.when(pl.program_id(2) == 0)
    def _():
        s_ref[...] = jnp.zeros_like(s_ref)

    head0 = lax.broadcasted_iota(jnp.int32, (1, PAIR), 1) < HEAD_DIM
    n2 = 2 * c_len
    row = lax.broadcasted_iota(jnp.int32, (n2, n2), 0)
    col = lax.broadcasted_iota(jnp.int32, (n2, n2), 1)
    t_loc, j_loc = row % c_len, col % c_len
    same_head = (row // c_len) == (col // c_len)
    strict = same_head & (j_loc < t_loc)
    incl = same_head & (j_loc <= t_loc)
    incl2 = jnp.concatenate([incl, incl], axis=1)
    sub_block = same_head & ((t_loc // 16) == (j_loc // 16))
    eye = (row == col).astype(F32)
    zeros = jnp.zeros((n2, PAIR), F32)
    trow = lax.broadcasted_iota(jnp.int32, (tt, tt), 0)
    tcol = lax.broadcasted_iota(jnp.int32, (tt, tt), 1)
    tri = ((trow >= tcol) & ((trow // c_len) == (tcol // c_len))).astype(BF16)

    r, lw, k, v, a, b = r_ref[...], lw_ref[...], k_ref[...], v_ref[...], a_ref[...], b_ref[...]
    cw = _dot_exact_lhs(tri, lw)
    e_pos, e_neg = jnp.exp(cw), jnp.exp(-cw)
    a_t, r_t, b_t, k_t = a * jnp.exp(cw - lw), r * e_pos, b * e_neg, k * e_neg

    def rows(x, c):
        return x[c * c_len:(c + 1) * c_len]

    e_last = [jnp.exp(rows(cw, c)[c_len - 1:c_len, :]) for c in chunks]
    a_s = [_stack_heads(rows(a_t, c), head0) for c in chunks]
    r_s = [_stack_heads(rows(r_t, c), head0) for c in chunks]
    v_s = [_stack_heads(rows(v, c), head0) for c in chunks]
    scores = [_dot_nt(jnp.concatenate([a_s[c], r_s[c]], axis=0).astype(BF16),
                      jnp.concatenate([_stack_heads(rows(b_t, c), head0),
                                       _stack_heads(rows(k_t, c), head0)], axis=0).astype(BF16)) for c in chunks]
    l_ab = [jnp.where(strict, s[:n2, :n2], 0.0) for s in scores]

    d1 = [jnp.where(sub_block, x, 0.0) for x in l_ab]
    d2 = [_dot_bf16(x, x) for x in d1]
    d4 = [_dot_bf16(x, x) for x in d2]
    d8 = [_dot_bf16(x, x) for x in d4]
    t_lo = [_dot_bf16(eye + x, eye + y) for x, y in zip(d1, d2)]
    t_hi = [_dot_bf16(eye + x, eye + y) for x, y in zip(d4, d8)]
    t_diag = [_dot_bf16(x, y) for x, y in zip(t_lo, t_hi)]
    n1 = [_dot_bf16(t, x - d) for t, x, d in zip(t_diag, l_ab, d1)]
    n_sq = [_dot_bf16(x, x) for x in n1]
    t_rest = [_dot_bf16(eye + x, eye + y) for x, y in zip(n1, n_sq)]
    t_inv = [_dot_bf16(x, t) for x, t in zip(t_rest, t_diag)]

    lv = [_dot_bf16(jnp.where(strict, s[:n2, n2:], 0.0), vs) for s, vs in zip(scores, v_s)]
    au = [_dot_bf16(t, jnp.concatenate([x, y], axis=1)) for t, x, y in zip(t_inv, a_s, lv)]
    z = []
    for c in chunks:
        e_end = e_last[c] * rows(e_neg, c)
        bend_s = _stack_heads(rows(b, c) * e_end, head0)
        kend_s = _stack_heads(rows(k, c) * e_end, head0)
        lhs = jnp.concatenate([jnp.where(incl2, scores[c][n2:], 0.0),
                               jnp.concatenate([bend_s.T, kend_s.T], axis=1)], axis=0)
        rhs = jnp.concatenate([au[c], jnp.concatenate([zeros, v_s[c]], axis=1)], axis=0)
        z.append(_dot_bf16(lhs, rhs))

    state = s_ref[...]
    for c in chunks:
        r_hat = r_s[c] + z[c][:n2, :PAIR]
        y_hat = z[c][:n2, PAIR:]
        lhs = jnp.concatenate([r_hat[:c_len] + r_hat[c_len:], z[c][n2:, :PAIR] + eye * e_last[c]], axis=0)
        prod = _dot_x3(lhs, state)
        y_ref[pl.ds(c * c_len, c_len), :] = prod[:c_len] + y_hat[:c_len] + y_hat[c_len:]
        state = prod[c_len:] + z[c][n2:, PAIR:]
    s_ref[...] = state

    ones = _head_sum_matrix().astype(BF16)
    y = y_ref[...]
    mu = _dot_exact_rhs(y, ones) * (1.0 / HEAD_DIM)
    d = y - mu
    var = _dot_exact_rhs(d * d, ones) * (1.0 / HEAD_DIM)
    yn = d * lax.rsqrt(var + RWKV_LN_EPS) * lnw_ref[...] + lnb_ref[...]
    bonus = _dot_exact_rhs(r * k * rk_ref[...], ones) * v
    o_ref[...] = ((yn + bonus) * g_ref[...]).astype(o_ref.dtype)


def rwkv_scan(r, lw, k, v, a, b, g, r_k, ln_w, ln_b, batch, seq, tt=512):
    m, dim = r.shape
    n_pairs = dim // PAIR
    nt = seq // tt
    seq_spec = pl.BlockSpec((tt, PAIR), lambda bb, p, i: (bb * nt + i, p))
    vec_spec = pl.BlockSpec((1, PAIR), lambda bb, p, i: (0, p))
    return pl.pallas_call(
        _rwkv_scan_kernel,
        grid=(batch, n_pairs, nt),
        in_specs=[seq_spec] * 7 + [vec_spec] * 3,
        out_specs=seq_spec,
        out_shape=jax.ShapeDtypeStruct((m, dim), BF16),
        scratch_shapes=[pltpu.VMEM((PAIR, PAIR), F32), pltpu.VMEM((tt, PAIR), F32)],
        compiler_params=_params(("parallel", "parallel", "arbitrary"), 32),
        name="rwkv_scan",
    )(r, lw, k, v, a, b, g, r_k, ln_w, ln_b)


def _pad_cols(w, width):
    return jnp.pad(w, ((0, 0), (0, width - w.shape[1])))


def _pad_rows(w, height):
    return jnp.pad(w, ((0, height - w.shape[0]), (0, 0)))


def _dup_kv_heads(w):
    d, c = w.shape
    return jnp.tile(w.reshape(d, c // HEAD_DIM, 1, HEAD_DIM), (1, 1, 2, 1)).reshape(d, 2 * c)


def even_mixer(h, xn, batch, seq, w_in, mix, w0, w2, a0, a2, g2, k_k, k_a, r_k, ln_w, ln_b, sink, w_out, layer,
               dim, swa_heads, swa_kv_heads, tm, tn):
    d_model = xn.shape[1]
    decay_lora, aaa_lora = w2.shape[0], a2.shape[0]
    o = 3 * dim
    o_a, o_g = o + decay_lora, o + decay_lora + aaa_lora
    rwkv_cols = o_g + g2.shape[0]
    scale = HEAD_DIM ** -0.5

    def rwkv_layout(t):
        return jnp.concatenate([t[..., :o], _pad_cols(t[..., o:o_a], LORA_PAD), _pad_cols(t[..., o_a:o_g], LORA_PAD),
                                t[..., o_g:rwkv_cols]], axis=-1)

    w_a = rwkv_layout(w_in).astype(BF16)
    q_end = rwkv_cols + swa_heads * HEAD_DIM
    k_end = q_end + swa_kv_heads * HEAD_DIM
    w_b = jnp.concatenate([w_in[:, rwkv_cols:q_end] * scale, _dup_kv_heads(w_in[:, q_end:k_end]),
                           _dup_kv_heads(w_in[:, k_end:])], axis=1).astype(BF16)

    pa = matmul_sum([(xn, d_model, 0)], [(w_a, d_model, 0)], w_a.shape[1], F32, tm, tn)
    pb = matmul_sum([(xn, d_model, 0)], [(w_b, d_model, 0)], w_b.shape[1], BF16, tm, tn)

    row = lambda t: t.reshape(1, -1)
    prep = rwkv_prep(pa, rwkv_layout(row(mix)), row(w0), _pad_rows(w2, LORA_PAD), row(a0), _pad_rows(a2, LORA_PAD),
                     g2, row(k_k), row(k_a), batch, seq, dim)
    y_a = rwkv_scan(*prep, row(r_k), row(ln_w), row(ln_b), batch, seq)

    q_pairs, kv_pairs = swa_heads // 2, swa_kv_heads
    kv_w = kv_pairs * PAIR
    kv_base = (q_pairs * PAIR) // kv_w
    sink_lanes = jnp.repeat(sink, HEAD_DIM).reshape(1, -1)
    y_b = swa_attention(pb, 0, kv_base, kv_base + 1, batch, seq, q_pairs, kv_pairs, SWA_WINDOW - 1, sink_lanes)

    return _out_proj(h, y_a, y_b, w_out, layer, tm, tn)


def odd_mixer(h, xn, batch, seq, w_in, b_f, w_out, layer, fox_heads, dil_heads, tm, tn, fox_tq):
    d_model = xn.shape[1]
    cw, dw = fox_heads * HEAD_DIM, dil_heads * HEAD_DIM
    scale = HEAD_DIM ** -0.5
    f0 = 3 * cw
    d0 = f0 + fox_heads
    w_c = jnp.concatenate([w_in[:, :cw] * scale, w_in[:, cw:f0]], axis=1).astype(BF16)
    w_d = jnp.concatenate([w_in[:, d0:d0 + dw] * scale, w_in[:, d0 + dw:]], axis=1).astype(BF16)
    w_f = _pad_cols(w_in[:, f0:d0], PAIR).astype(BF16)
    proj = matmul_sum([(xn, d_model, 0)], [(w_c, d_model, 0)], f0, BF16, tm, tn)
    proj_d = matmul_sum([(xn, d_model, 0)], [(w_d, d_model, 0)], 3 * dw, F32, tm, tn)
    fc = matmul_sum([(xn, d_model, 0)], [(w_f, d_model, 0)], PAIR, F32, tm, PAIR)

    c = forget_cumsum(fc, _pad_cols(b_f.reshape(1, -1), PAIR), batch, seq)
    c = c[:, :fox_heads].reshape(batch, seq, fox_heads).transpose(0, 2, 1)
    fox_pairs = fox_heads // 2
    y_c = fox_attention(proj, c[..., None], c[:, :, None, :], batch, seq, fox_pairs, 0, fox_pairs, 2 * fox_pairs,
                        fox_tq)

    y_d = dilated_attention(proj_d, batch, seq, dil_heads // 2)

    return _out_proj(h, y_c, y_d, w_out, layer, tm, tn)


def _out_proj(h, y_1, y_2, w_out, layer, tm, tn):
    w1, w2 = y_1.shape[1], y_2.shape[1]
    kw = math.gcd(w1, w2)
    n1, n2 = w1 // kw, w2 // kw
    a_ops = [(y_1, kw, c) for c in range(n1)] + [(y_2, kw, c) for c in range(n2)]
    b_ops = [(w_out, kw, c) for c in range(n1 + n2)]
    return matmul_sum(a_ops, b_ops, w_out.shape[2], F32, tm, tn, residual=h, layer=layer)


def ffn(h, xn, w_gate, w_up, w_down, layer, tm, tn_hidden, tm_down, tn_down):
    half = w_gate.shape[2] // 2
    hidden = swiglu_hidden(xn, w_gate, w_up, layer, tm, tn_hidden)
    return matmul_sum([(hidden, half, 0), (hidden, half, 1)], [(w_down, half, 0), (w_down, half, 1)],
                      w_down.shape[2], F32, tm_down, tn_down, residual=h, layer=layer, vmem_mib=56)


def kernel(x, norm_mix, norm_ffn, norm_final, ffn_w_gate, ffn_w_up, ffn_w_down, ev_w_in, rwkv_mix, rwkv_w0, rwkv_w2, rwkv_a0, rwkv_a2, rwkv_g2, rwkv_k_k, rwkv_k_a, rwkv_r_k, rwkv_ln_w, rwkv_ln_b, swa_sink, ev_w_out, od_w_in, fox_b_f, od_w_out):
    batch, seq, d_model = x.shape
    depth = norm_mix.shape[0]
    dim = rwkv_w0.shape[1]
    swa_heads = swa_sink.shape[1]
    fox_heads = fox_b_f.shape[1]
    dil_heads = d_model // HEAD_DIM - fox_heads
    tm, tn = 1024, 512
    w_gate, w_up, w_down = ffn_w_gate.astype(BF16), ffn_w_up.astype(BF16), ffn_w_down.astype(BF16)
    w_out_even, w_out_odd = ev_w_out.astype(BF16), od_w_out.astype(BF16)
    h = x.reshape(batch * seq, d_model)
    for layer in range(depth):
        i = layer // 2
        xn = rmsnorm(h, norm_mix[layer], BF16)
        if layer % 2 == 0:
            h = even_mixer(h, xn, batch, seq, ev_w_in[i], rwkv_mix[i], rwkv_w0[i], rwkv_w2[i], rwkv_a0[i], rwkv_a2[i],
                           rwkv_g2[i], rwkv_k_k[i], rwkv_k_a[i], rwkv_r_k[i].reshape(-1), rwkv_ln_w[i], rwkv_ln_b[i],
                           swa_sink[i], w_out_even, i, dim, swa_heads, swa_heads // 8, tm, tn)
        else:
            h = odd_mixer(h, xn, batch, seq, od_w_in[i], fox_b_f[i], w_out_odd, i, fox_heads, dil_heads, tm, tn, 512)
        xn = rmsnorm(h, norm_ffn[layer], BF16)
        h = ffn(h, xn, w_gate, w_up, w_down, layer, tm, 256, 512, 256)
    return rmsnorm(h, norm_final, F32).reshape(batch, seq, d_model)
```

```python
import functools
import math

import jax
import jax.numpy as jnp
from jax import lax
from jax.experimental import pallas as pl
from jax.experimental.pallas import tpu as pltpu

F32 = jnp.float32
BF16 = jnp.bfloat16
HIGHEST = lax.Precision.HIGHEST

HEAD_DIM = 64
PAIR = 2 * HEAD_DIM
BLOCK = 128
SWA_WINDOW = 128
DILATED_PAIRS = ((128, 1), (512, 4), (2048, 16))
RMS_EPS = 1e-6
RWKV_LN_EPS = 64e-5
NEG_INF = -1e30
DILATED_GROUP = 4
RWKV_CHUNK = 64
LORA_PAD = 128
MIB = 1024 * 1024


def _params(semantics, vmem_mib):
    return pltpu.CompilerParams(dimension_semantics=semantics, vmem_limit_bytes=vmem_mib * MIB)


def _dot(a, b, precision=None):
    return jnp.dot(a, b, preferred_element_type=F32, precision=precision)


def _dot_bf16(a, b):
    return jnp.dot(a.astype(BF16), b.astype(BF16), preferred_element_type=F32)


def _dot_nt(a, b, precision=None):
    return lax.dot_general(a, b, (((1,), (1,)), ((), ())), preferred_element_type=F32, precision=precision)


def _split3(x):
    hi = x.astype(BF16)
    rem = x - hi.astype(F32)
    mid = rem.astype(BF16)
    lo = (rem - mid.astype(F32)).astype(BF16)
    return hi, mid, lo


def _dot_exact_lhs(m, x):
    hi, mid, lo = _split3(x)
    return _dot(m, lo) + _dot(m, mid) + _dot(m, hi)


def _dot_exact_rhs(x, m):
    hi, mid, lo = _split3(x)
    return _dot(lo, m) + _dot(mid, m) + _dot(hi, m)


def _dot_x3(a, b):
    a_hi, b_hi = a.astype(BF16), b.astype(BF16)
    a_lo, b_lo = (a - a_hi.astype(F32)).astype(BF16), (b - b_hi.astype(F32)).astype(BF16)
    return _dot(a_lo, b_hi) + _dot(a_hi, b_lo) + _dot(a_hi, b_hi)


def _sigmoid(x):
    return 1.0 / (1.0 + jnp.exp(-x))


def _softplus(x):
    return jnp.maximum(x, 0.0) + jnp.log(1.0 + jnp.exp(-jnp.abs(x)))


def _tile(n, target):
    t = min(n, target) // PAIR * PAIR
    while n % t:
        t -= PAIR
    return t


def _head_sum_matrix():
    r = lax.broadcasted_iota(jnp.int32, (PAIR, PAIR), 0) // HEAD_DIM
    c = lax.broadcasted_iota(jnp.int32, (PAIR, PAIR), 1) // HEAD_DIM
    return (r == c).astype(F32)


def _rmsnorm_kernel(x_ref, g_ref, o_ref):
    x = x_ref[...]
    ms = jnp.mean(x * x, axis=-1, keepdims=True)
    o_ref[...] = (x * lax.rsqrt(ms + RMS_EPS) * g_ref[...]).astype(o_ref.dtype)


def rmsnorm(x, g, out_dtype, tm=256):
    m, d = x.shape
    return pl.pallas_call(
        _rmsnorm_kernel,
        grid=(m // tm,),
        in_specs=[pl.BlockSpec((tm, d), lambda i: (i, 0)), pl.BlockSpec((1, d), lambda i: (0, 0))],
        out_specs=pl.BlockSpec((tm, d), lambda i: (i, 0)),
        out_shape=jax.ShapeDtypeStruct((m, d), out_dtype),
        compiler_params=_params(("parallel",), 40),
        name="rmsnorm",
    )(x, g.reshape(1, d))


def _matmul_sum_kernel(*refs, n_ops, has_res):
    a_refs, b_refs = refs[:n_ops], refs[n_ops:2 * n_ops]
    o_ref = refs[-1]
    acc = _dot(a_refs[0][...], b_refs[0][...])
    for a_ref, b_ref in zip(a_refs[1:], b_refs[1:]):
        acc = acc + _dot(a_ref[...], b_ref[...])
    if has_res:
        acc = acc + refs[2 * n_ops][...]
    o_ref[...] = acc.astype(o_ref.dtype)


def matmul_sum(a_ops, b_ops, n_out, out_dtype, tm, tn, residual=None, layer=None, vmem_mib=48):
    m = a_ops[0][0].shape[0]
    tm, tn = _tile(m, tm), _tile(n_out, tn)
    in_specs, args = [], []
    for arr, kw, kb in a_ops:
        in_specs.append(pl.BlockSpec((tm, kw), lambda i, j, kb=kb: (i, kb)))
        args.append(arr)
    for arr, kw, kb in b_ops:
        if layer is None:
            in_specs.append(pl.BlockSpec((kw, tn), lambda i, j, kb=kb: (kb, j)))
        else:
            in_specs.append(pl.BlockSpec((None, kw, tn), lambda i, j, kb=kb: (layer, kb, j)))
        args.append(arr)
    if residual is not None:
        in_specs.append(pl.BlockSpec((tm, tn), lambda i, j: (i, j)))
        args.append(residual)
    return pl.pallas_call(
        functools.partial(_matmul_sum_kernel, n_ops=len(a_ops), has_res=residual is not None),
        grid=(m // tm, n_out // tn),
        in_specs=in_specs,
        out_specs=pl.BlockSpec((tm, tn), lambda i, j: (i, j)),
        out_shape=jax.ShapeDtypeStruct((m, n_out), out_dtype),
        compiler_params=_params(("parallel", "parallel"), vmem_mib),
        name="matmul_sum",
    )(*args)


def _swiglu_kernel(x_ref, wg_ref, wu_ref, o_ref):
    x = x_ref[...]
    gate = _dot(x, wg_ref[...])
    up = _dot(x, wu_ref[...])
    o_ref[...] = (gate * _sigmoid(gate) * up).astype(o_ref.dtype)


def swiglu_hidden(x, wg, wu, layer, tm, tn):
    m, k = x.shape
    f = wg.shape[2]
    tm, tn = _tile(m, tm), _tile(f, tn)
    w_spec = pl.BlockSpec((None, k, tn), lambda i, j: (layer, 0, j))
    return pl.pallas_call(
        _swiglu_kernel,
        grid=(m // tm, f // tn),
        in_specs=[pl.BlockSpec((tm, k), lambda i, j: (i, 0)), w_spec, w_spec],
        out_specs=pl.BlockSpec((tm, tn), lambda i, j: (i, j)),
        out_shape=jax.ShapeDtypeStruct((m, f), BF16),
        compiler_params=_params(("parallel", "parallel"), 48),
        name="swiglu_hidden",
    )(x, wg, wu)


def _band_mask(max_dist, has_prev):
    row = lax.broadcasted_iota(jnp.int32, (BLOCK, 2 * BLOCK), 0)
    col = lax.broadcasted_iota(jnp.int32, (BLOCK, 2 * BLOCK), 1)
    dist = row + BLOCK - col
    return (dist >= 0) & (dist <= max_dist) & ((col >= BLOCK) | has_prev)


def _pair_attention_group(qs, ks, vs, mask, head0, sinks=None):
    g = len(qs)
    masked_q = [jnp.where(head0 if e == 0 else jnp.logical_not(head0), q, jnp.zeros_like(q))
                for q in qs for e in range(2)]
    if len(ks) == 1:
        s = _dot_nt(jnp.concatenate(masked_q, axis=0), ks[0]).reshape(2 * g, BLOCK, 2 * BLOCK)
    else:
        s = jnp.stack([_dot_nt(q, ks[u // 2]) for u, q in enumerate(masked_q)])
    s = jnp.where(mask[None], s, NEG_INF)
    mx = jnp.max(s, axis=-1, keepdims=True)
    if sinks is not None:
        sk = jnp.stack(sinks)
        mx = jnp.maximum(mx, sk)
    pe = jnp.exp(s - mx)
    den = jnp.sum(pe, axis=-1, keepdims=True)
    if sinks is not None:
        den = den + jnp.exp(sk - mx)
    probs = pe.astype(BF16)
    inv_den = 1.0 / den
    lse = mx + jnp.log(den)

    def stack_heads(v):
        return jnp.concatenate([jnp.where(head0, v, jnp.zeros_like(v)), jnp.where(head0, jnp.zeros_like(v), v)], axis=0)

    p_cat = [jnp.concatenate([probs[2 * u], probs[2 * u + 1]], axis=1) for u in range(g)]
    if len(vs) == 1:
        o_all = _dot(jnp.concatenate(p_cat, axis=0), stack_heads(vs[0]))
        outs = [o_all[u * BLOCK:(u + 1) * BLOCK] for u in range(g)]
    else:
        outs = [_dot(p, stack_heads(v)) for p, v in zip(p_cat, vs)]
    return [(o * jnp.where(head0, inv_den[2 * u], inv_den[2 * u + 1]), jnp.where(head0, lse[2 * u], lse[2 * u + 1]))
            for u, o in enumerate(outs)]


def _swa_kernel(q_ref, kc_ref, kp_ref, vc_ref, vp_ref, sink_ref, o_ref, *, n_pairs, rep, max_dist):
    mask = _band_mask(max_dist, pl.program_id(1) > 0)
    head0 = lax.broadcasted_iota(jnp.int32, (1, PAIR), 1) < HEAD_DIM
    for kv in range(n_pairs // rep):
        kl = pl.ds(kv * PAIR, PAIR)
        k = jnp.concatenate([kp_ref[:, kl], kc_ref[:, kl]], axis=0)
        v = jnp.concatenate([vp_ref[:, kl], vc_ref[:, kl]], axis=0)
        pairs = range(kv * rep, (kv + 1) * rep)
        sinks = [sink_ref[0:1, pl.ds(p * PAIR + e * HEAD_DIM, 1)] for p in pairs for e in range(2)]
        outs = _pair_attention_group([q_ref[:, pl.ds(p * PAIR, PAIR)] for p in pairs], [k], [v], mask, head0, sinks)
        for p, (o, _) in zip(pairs, outs):
            o_ref[:, pl.ds(p * PAIR, PAIR)] = o.astype(o_ref.dtype)


def swa_attention(qkv, q_col, k_col, v_col, batch, seq, n_pairs, n_kv_pairs, max_dist, sink):
    nb = seq // BLOCK
    qw, kw = n_pairs * PAIR, n_kv_pairs * PAIR
    cur = lambda c: (lambda b, i: (b * nb + i, c))
    prev = lambda c: (lambda b, i: (b * nb + jnp.maximum(i - 1, 0), c))
    return pl.pallas_call(
        functools.partial(_swa_kernel, n_pairs=n_pairs, rep=n_pairs // n_kv_pairs, max_dist=max_dist),
        grid=(batch, nb),
        in_specs=[pl.BlockSpec((BLOCK, qw), cur(q_col)),
                  pl.BlockSpec((BLOCK, kw), cur(k_col)), pl.BlockSpec((BLOCK, kw), prev(k_col)),
                  pl.BlockSpec((BLOCK, kw), cur(v_col)), pl.BlockSpec((BLOCK, kw), prev(v_col)),
                  pl.BlockSpec((1, qw), lambda b, i: (0, 0))],
        out_specs=pl.BlockSpec((BLOCK, qw), cur(0)),
        out_shape=jax.ShapeDtypeStruct((batch * seq, qw), BF16),
        compiler_params=_params(("parallel", "parallel"), 32),
        name="swa_attention",
    )(qkv, qkv, qkv, qkv, qkv, sink)


def _dilated_kernel(q_ref, kc_ref, kp_ref, vc_ref, vp_ref, y_ref, o_ref, lse_ref):
    slab = q_ref.shape[0]
    not_first = pl.program_id(1) > 0
    head0 = lax.broadcasted_iota(jnp.int32, (1, PAIR), 1) < HEAD_DIM
    for branch, (window, dil) in enumerate(DILATED_PAIRS):
        span = BLOCK * dil
        inner_mask = _band_mask(window // dil, True)
        edge_mask = _band_mask(window // dil, not_first)
        units = [(blk, p) for blk in range(slab // span) for p in range(dil)]
        edge_units = [u for u in units if u[0] == 0]
        inner_units = [u for u in units if u[0] > 0]
        groups = [(edge_units[g:g + DILATED_GROUP], edge_mask) for g in range(0, len(edge_units), DILATED_GROUP)]
        groups += [(inner_units[g:g + DILATED_GROUP], inner_mask) for g in range(0, len(inner_units), DILATED_GROUP)]
        for group, mask in groups:
            qs, ks, vs, rows = [], [], [], []
            for blk, p in group:
                cur = pl.ds(blk * span + p, BLOCK, stride=dil)
                if blk:
                    before = pl.ds((blk - 1) * span + p, BLOCK, stride=dil)
                    k_prev, v_prev = kc_ref[before, :], vc_ref[before, :]
                else:
                    before = pl.ds(slab - span + p, BLOCK, stride=dil)
                    k_prev, v_prev = kp_ref[before, :], vp_ref[before, :]
                qs.append(q_ref[cur, :].astype(BF16))
                ks.append(jnp.concatenate([k_prev, kc_ref[cur, :]], axis=0).astype(BF16))
                vs.append(jnp.concatenate([v_prev, vc_ref[cur, :]], axis=0).astype(BF16))
                rows.append(cur)
            for cur, (o, lse) in zip(rows, _pair_attention_group(qs, ks, vs, mask, head0)):
                o_ref[branch, cur, :] = o
                lse_ref[branch, cur, :] = lse
    l1, l2, l3 = lse_ref[0], lse_ref[1], lse_ref[2]
    mx = jnp.maximum(jnp.maximum(l1, l2), l3)
    e1, e2, e3 = jnp.exp(l1 - mx), jnp.exp(l2 - mx), jnp.exp(l3 - mx)
    num = e1 * o_ref[0] + e2 * o_ref[1] + e3 * o_ref[2]
    y_ref[...] = (num / (e1 + e2 + e3)).astype(y_ref.dtype)


def dilated_attention(qkv, batch, seq, n_pairs):
    slab = BLOCK * max(dil for _, dil in DILATED_PAIRS)
    ns = seq // slab
    cur = lambda base: (lambda b, s, p: (b * ns + s, base + p))
    prev = lambda base: (lambda b, s, p: (b * ns + jnp.maximum(s - 1, 0), base + p))
    block = lambda index_map: pl.BlockSpec((slab, PAIR), index_map)
    n_branches = len(DILATED_PAIRS)
    return pl.pallas_call(
        _dilated_kernel,
        grid=(batch, ns, n_pairs),
        in_specs=[block(cur(0)), block(cur(n_pairs)), block(prev(n_pairs)),
                  block(cur(2 * n_pairs)), block(prev(2 * n_pairs))],
        out_specs=block(cur(0)),
        out_shape=jax.ShapeDtypeStruct((batch * seq, n_pairs * PAIR), BF16),
        scratch_shapes=[pltpu.VMEM((n_branches, slab, PAIR), F32), pltpu.VMEM((n_branches, slab, PAIR), F32)],
        compiler_params=_params(("parallel", "parallel", "parallel"), 40),
        name="dilated_attention",
    )(qkv, qkv, qkv, qkv, qkv)


def _forget_cumsum_kernel(f_ref, b_ref, c_ref, kb_ref, carry_ref):
    @pl.when(pl.program_id(1) == 0)
    def _():
        carry_ref[...] = jnp.zeros_like(carry_ref)

    blk = f_ref.shape[0]
    log_f = -_softplus(-(f_ref[...] + b_ref[...]))
    tri = (lax.broadcasted_iota(jnp.int32, (blk, blk), 0) >= lax.broadcasted_iota(jnp.int32, (blk, blk), 1))
    c = _dot_exact_lhs(tri.astype(BF16), log_f) + carry_ref[...]
    c_ref[...] = c
    carry_ref[...] = c[blk - 1:blk, :]

    width = kb_ref.shape[1]
    head = lax.broadcasted_iota(jnp.int32, (c.shape[1], width), 0)
    out = lax.broadcasted_iota(jnp.int32, (c.shape[1], width), 1)
    pair, lane = out // PAIR, out % PAIR
    placed = None
    for piece, part in enumerate(_split3(-c)):
        select = ((head == 2 * pair) & (lane == piece)) | ((head == 2 * pair + 1) & (lane == 3 + piece))
        term = _dot(part, select.astype(BF16))
        placed = term if placed is None else placed + term
    kb_ref[...] = placed.astype(BF16)


def forget_cumsum(fc, b_f, batch, seq, n_pairs, blk=512):
    w = fc.shape[1]
    nb = seq // blk
    row_block = lambda width: pl.BlockSpec((blk, width), lambda b, i: (b * nb + i, 0))
    return pl.pallas_call(
        _forget_cumsum_kernel,
        grid=(batch, nb),
        in_specs=[row_block(w), pl.BlockSpec((1, w), lambda b, i: (0, 0))],
        out_specs=[row_block(w), row_block(n_pairs * PAIR)],
        out_shape=[jax.ShapeDtypeStruct(fc.shape, F32), jax.ShapeDtypeStruct((fc.shape[0], n_pairs * PAIR), BF16)],
        scratch_shapes=[pltpu.VMEM((1, w), F32)],
        compiler_params=_params(("parallel", "arbitrary"), 16),
        name="forget_cumsum",
    )(fc, b_f)


FOX_ROWS = 32
FOX_PAIRS = 4


def _fox_kernel(qi_ref, kj_ref, q_ref, k_ref, v_ref, cb_ref, cq_ref, o_ref,
                m_ref, al_ref, cqr_ref, acc_ref, s_ref, p_ref):
    t = pl.program_id(2)
    i, j = qi_ref[t], kj_ref[t]
    tq, tk = q_ref.shape[0], k_ref.shape[0]
    n_tiles = tk // PAIR
    heads = range(2 * FOX_PAIRS)
    lane1 = lax.broadcasted_iota(jnp.int32, (1, PAIR), 1)
    head0 = lane1 < HEAD_DIM

    @pl.when(j == 0)
    def _():
        m_ref[...] = jnp.full_like(m_ref, NEG_INF)
        acc_ref[...] = jnp.zeros_like(acc_ref)
        for h in heads:
            cqr_ref[h] = jnp.broadcast_to(cq_ref[0, h], (tq, PAIR))

    def step(masked):
        lane = lax.broadcasted_iota(jnp.int32, (FOX_ROWS, PAIR), 1)
        sub = lax.broadcasted_iota(jnp.int32, (FOX_ROWS, PAIR), 0)
        for h in heads:
            g, e = divmod(h, 2)
            pair = pl.ds(g * PAIR, PAIR)
            q = q_ref[:, pair]
            sel = head0 if e == 0 else jnp.logical_not(head0)
            bias_on = jnp.broadcast_to(((lane1 >= 3 * e) & (lane1 < 3 * e + 3)).astype(BF16), (tq, PAIR))
            s_ref[h] = _dot_nt(jnp.concatenate([jnp.where(sel, q, jnp.zeros_like(q)), bias_on], axis=1),
                               jnp.concatenate([k_ref[:, pair], cb_ref[:, pair]], axis=1))
        pv = []
        for h in heads:
            g, e = divmod(h, 2)
            sel = head0 if e == 0 else jnp.logical_not(head0)
            for c in range(tq // FOX_ROWS):
                rows = pl.ds(c * FOX_ROWS, FOX_ROWS)
                tiles = []
                for n in range(n_tiles):
                    s = s_ref[h, rows, pl.ds(n * PAIR, PAIR)]
                    if masked:
                        s = jnp.where(lane + n * PAIR <= sub + c * FOX_ROWS, s, NEG_INF)
                    tiles.append(s)
                mx = tiles[0]
                for s in tiles[1:]:
                    mx = jnp.maximum(mx, s)
                cq = cqr_ref[h, rows, :]
                m_prev = m_ref[h, rows, :]
                m_new = jnp.maximum(m_prev, jnp.max(mx, axis=-1, keepdims=True) + cq)
                shift = cq - m_new
                for n, s in enumerate(tiles):
                    p_ref[rows, pl.ds(h * tk + n * PAIR, PAIR)] = jnp.exp(s + shift).astype(BF16)
                m_ref[h, rows, :] = m_new
                al_ref[h, rows, :] = jnp.exp(m_prev - m_new)
            v = v_ref[:, pl.ds(g * PAIR, PAIR)]
            v_aug = jnp.concatenate([jnp.where(sel, v, jnp.zeros_like(v)),
                                     jnp.broadcast_to((lane1 == e).astype(BF16), (tk, PAIR))], axis=1)
            pv.append(_dot(p_ref[:, pl.ds(h * tk, tk)], v_aug))
        for g in range(FOX_PAIRS):
            a0, a1 = al_ref[2 * g], al_ref[2 * g + 1]
            alpha = jnp.concatenate([jnp.where(head0, a0, a1), jnp.where(lane1 == 0, a0, a1)], axis=1)
            acc = acc_ref[g] * alpha + pv[2 * g] + pv[2 * g + 1]
            if masked:
                den = jnp.where(head0, acc[:, PAIR:PAIR + 1], acc[:, PAIR + 1:PAIR + 2])
                o_ref[:, pl.ds(g * PAIR, PAIR)] = (acc[:, :PAIR] / den).astype(o_ref.dtype)
            else:
                acc_ref[g] = acc

    @pl.when(j < i)
    def _():
        step(False)

    @pl.when(j == i)
    def _():
        step(True)


def fox_attention(qkv, key_bias, c_col, batch, seq, n_pairs, q_col, k_col, v_col, tq):
    nq = seq // tq
    qi = jnp.asarray([i for i in range(nq) for _ in range(i + 1)], jnp.int32)
    kj = jnp.asarray([j for i in range(nq) for j in range(i + 1)], jnp.int32)
    gw, heads = FOX_PAIRS * PAIR, 2 * FOX_PAIRS
    rep = pltpu.VMEM((heads, tq, PAIR), F32)
    grid_spec = pltpu.PrefetchScalarGridSpec(
        num_scalar_prefetch=2,
        grid=(batch, n_pairs // FOX_PAIRS, qi.shape[0]),
        in_specs=[pl.BlockSpec((tq, gw), lambda b, p, t, qi, kj: (b * nq + qi[t], q_col // FOX_PAIRS + p)),
                  pl.BlockSpec((tq, gw), lambda b, p, t, qi, kj: (b * nq + kj[t], k_col // FOX_PAIRS + p)),
                  pl.BlockSpec((tq, gw), lambda b, p, t, qi, kj: (b * nq + kj[t], v_col // FOX_PAIRS + p)),
                  pl.BlockSpec((tq, gw), lambda b, p, t, qi, kj: (b * nq + kj[t], p)),
                  pl.BlockSpec((1, heads, tq, 1), lambda b, p, t, qi, kj: (b, p, qi[t], 0))],
        out_specs=pl.BlockSpec((tq, gw), lambda b, p, t, qi, kj: (b * nq + qi[t], p)),
        scratch_shapes=[rep, rep, rep, pltpu.VMEM((FOX_PAIRS, tq, 2 * PAIR), F32),
                        pltpu.VMEM((heads, tq, tq), F32), pltpu.VMEM((tq, heads * tq), BF16)])
    return pl.pallas_call(
        _fox_kernel,
        grid_spec=grid_spec,
        out_shape=jax.ShapeDtypeStruct((batch * seq, n_pairs * PAIR), BF16),
        compiler_params=_params(("parallel", "parallel", "arbitrary"), 32),
        name="fox_attention",
    )(qi, kj, qkv, qkv, qkv, key_bias, c_col)


def _rwkv_prep_kernel(x_ref, xp_ref, mix_ref, w0_ref, w2_ref, a0_ref, a2_ref, g2_ref, kk_ref, ka_ref,
                      r_ref, lw_ref, k_ref, v_ref, a_ref, b_ref, g_ref, *, dim):
    first = pl.program_id(1) == 0
    x = x_ref[...]
    tt = x.shape[0]
    prev_row = jnp.where(first, 0.0, xp_ref[7:8, :])
    row = lax.broadcasted_iota(jnp.int32, (tt, 1), 0)
    shifted = jnp.where(row == 0, prev_row, pltpu.roll(x, 1, 0))
    x = x + (shifted - x) * mix_ref[...]

    r, k, v = x[:, 0:dim], x[:, dim:2 * dim], x[:, 2 * dim:3 * dim]
    o = 3 * dim
    xw, xa, xg = x[:, o:o + LORA_PAD], x[:, o + LORA_PAD:o + 2 * LORA_PAD], x[:, o + 2 * LORA_PAD:]
    w_log = -_softplus(-(w0_ref[...] + _dot_x3(jnp.tanh(xw), w2_ref[...]))) - 0.5
    a = _sigmoid(a0_ref[...] + _dot_x3(xa, a2_ref[...]))
    g = _dot_x3(_sigmoid(xg), g2_ref[...])

    kk = k * kk_ref[...]
    ones = _head_sum_matrix().astype(BF16)
    norm = []
    for p in range(dim // PAIR):
        kp = kk[:, p * PAIR:(p + 1) * PAIR]
        norm.append(jnp.sqrt(_dot_exact_rhs(kp * kp, ones)))
    kk = kk / jnp.maximum(jnp.concatenate(norm, axis=1), 1e-12)

    r_ref[...] = r
    lw_ref[...] = -jnp.exp(w_log)
    k_ref[...] = k * (1.0 + (a - 1.0) * ka_ref[...])
    v_ref[...] = v
    a_ref[...] = -kk
    b_ref[...] = kk * a
    g_ref[...] = g


def rwkv_prep(pa, mix, w0, w2, a0, a2, g2, k_k, k_a, batch, seq, dim, tt=128):
    m, cols = pa.shape
    nt = seq // tt
    full = lambda arr: pl.BlockSpec(arr.shape, lambda b, i: (0, 0))
    out_spec = pl.BlockSpec((tt, dim), lambda b, i: (b * nt + i, 0))
    consts = [mix, w0, w2, a0, a2, g2, k_k, k_a]
    return pl.pallas_call(
        functools.partial(_rwkv_prep_kernel, dim=dim),
        grid=(batch, nt),
        in_specs=[pl.BlockSpec((tt, cols), lambda b, i: (b * nt + i, 0)),
                  pl.BlockSpec((8, cols), lambda b, i: (jnp.maximum((b * nt + i) * (tt // 8) - 1, 0), 0))]
                 + [full(c) for c in consts],
        out_specs=[out_spec] * 7,
        out_shape=[jax.ShapeDtypeStruct((m, dim), F32)] * 7,
        compiler_params=_params(("parallel", "parallel"), 48),
        name="rwkv_prep",
    )(pa, pa, *consts)


def _stack_heads(x, head0):
    zero = jnp.zeros_like(x)
    return jnp.concatenate([jnp.where(head0, x, zero), jnp.where(head0, zero, x)], axis=0)


def _rwkv_scan_kernel(r_ref, lw_ref, k_ref, v_ref, a_ref, b_ref, g_ref, rk_ref, lnw_ref, lnb_ref,
                      o_ref, s_ref, y_ref):
    c_len = RWKV_CHUNK
    tt = r_ref.shape[0]
    chunks = range(tt // c_len)

    @pl.when(pl.program_id(2) == 0)
    def _():
        s_ref[...] = jnp.zeros_like(s_ref)

    head0 = lax.broadcasted_iota(jnp.int32, (1, PAIR), 1) < HEAD_DIM
    n2 = 2 * c_len
    row = lax.broadcasted_iota(jnp.int32, (n2, n2), 0)
    col = lax.broadcasted_iota(jnp.int32, (n2, n2), 1)
    t_loc, j_loc = row % c_len, col % c_len
    same_head = (row // c_len) == (col // c_len)
    strict = same_head & (j_loc < t_loc)
    incl = same_head & (j_loc <= t_loc)
    incl2 = jnp.concatenate([incl, incl], axis=1)
    sub_block = same_head & ((t_loc // 16) == (j_loc // 16))
    eye = (row == col).astype(F32)
    zeros = jnp.zeros((n2, PAIR), F32)
    tri = (lax.broadcasted_iota(jnp.int32, (c_len, c_len), 0)
           >= lax.broadcasted_iota(jnp.int32, (c_len, c_len), 1)).astype(BF16)

    def rows(x, c):
        return x[c * c_len:(c + 1) * c_len]

    r, lw, k, v, a, b = r_ref[...], lw_ref[...], k_ref[...], v_ref[...], a_ref[...], b_ref[...]
    cw = jnp.concatenate([_dot_exact_lhs(tri, rows(lw, c)) for c in chunks], axis=0)
    e_pos, e_neg = jnp.exp(cw), jnp.exp(-cw)
    a_t, r_t, b_t, k_t = a * jnp.exp(cw - lw), r * e_pos, b * e_neg, k * e_neg

    e_last = [jnp.exp(rows(cw, c)[c_len - 1:c_len, :]) for c in chunks]
    a_s = [_stack_heads(rows(a_t, c), head0) for c in chunks]
    r_s = [_stack_heads(rows(r_t, c), head0) for c in chunks]
    v_s = [_stack_heads(rows(v, c), head0) for c in chunks]
    scores = [_dot_nt(jnp.concatenate([a_s[c], r_s[c]], axis=0).astype(BF16),
                      jnp.concatenate([_stack_heads(rows(b_t, c), head0),
                                       _stack_heads(rows(k_t, c), head0)], axis=0).astype(BF16)) for c in chunks]
    l_ab = [jnp.where(strict, s[:n2, :n2], 0.0) for s in scores]

    d1 = [jnp.where(sub_block, x, 0.0) for x in l_ab]
    d2 = [_dot_bf16(x, x) for x in d1]
    d4 = [_dot_bf16(x, x) for x in d2]
    d8 = [_dot_bf16(x, x) for x in d4]
    t_lo = [_dot_bf16(eye + x, eye + y) for x, y in zip(d1, d2)]
    t_hi = [_dot_bf16(eye + x, eye + y) for x, y in zip(d4, d8)]
    t_diag = [_dot_bf16(x, y) for x, y in zip(t_lo, t_hi)]
    n1 = [_dot_bf16(t, x - d) for t, x, d in zip(t_diag, l_ab, d1)]
    n_sq = [_dot_bf16(x, x) for x in n1]
    t_rest = [_dot_bf16(eye + x, eye + y) for x, y in zip(n1, n_sq)]
    t_inv = [_dot_bf16(x, t) for x, t in zip(t_rest, t_diag)]

    lv = [_dot_bf16(jnp.where(strict, s[:n2, n2:], 0.0), vs) for s, vs in zip(scores, v_s)]
    au = [_dot_bf16(t, jnp.concatenate([x, y], axis=1)) for t, x, y in zip(t_inv, a_s, lv)]
    z = []
    for c in chunks:
        e_end = e_last[c] * rows(e_neg, c)
        bend_s = _stack_heads(rows(b, c) * e_end, head0)
        kend_s = _stack_heads(rows(k, c) * e_end, head0)
        lhs = jnp.concatenate([jnp.where(incl2, scores[c][n2:], 0.0),
                               jnp.concatenate([bend_s.T, kend_s.T], axis=1)], axis=0)
        rhs = jnp.concatenate([au[c], jnp.concatenate([zeros, v_s[c]], axis=1)], axis=0)
        z.append(_dot_bf16(lhs, rhs))

    state = s_ref[...]
    for c in chunks:
        r_hat = r_s[c] + z[c][:n2, :PAIR]
        y_hat = z[c][:n2, PAIR:]
        lhs = jnp.concatenate([r_hat[:c_len] + r_hat[c_len:], z[c][n2:, :PAIR] + eye * e_last[c]], axis=0)
        prod = _dot_bf16(lhs, state)
        y_ref[pl.ds(c * c_len, c_len), :] = prod[:c_len] + y_hat[:c_len] + y_hat[c_len:]
        state = prod[c_len:] + z[c][n2:, PAIR:]
    s_ref[...] = state

    ones = _head_sum_matrix().astype(BF16)
    y = y_ref[...]
    mu = _dot_exact_rhs(y, ones) * (1.0 / HEAD_DIM)
    d = y - mu
    var = _dot_exact_rhs(d * d, ones) * (1.0 / HEAD_DIM)
    yn = d * lax.rsqrt(var + RWKV_LN_EPS) * lnw_ref[...] + lnb_ref[...]
    bonus = _dot_exact_rhs(r * k * rk_ref[...], ones) * v
    o_ref[...] = ((yn + bonus) * g_ref[...]).astype(o_ref.dtype)


def rwkv_scan(r, lw, k, v, a, b, g, r_k, ln_w, ln_b, batch, seq, tt=512):
    m, dim = r.shape
    n_pairs = dim // PAIR
    nt = seq // tt
    seq_spec = pl.BlockSpec((tt, PAIR), lambda bb, p, i: (bb * nt + i, p))
    vec_spec = pl.BlockSpec((1, PAIR), lambda bb, p, i: (0, p))
    return pl.pallas_call(
        _rwkv_scan_kernel,
        grid=(batch, n_pairs, nt),
        in_specs=[seq_spec] * 7 + [vec_spec] * 3,
        out_specs=seq_spec,
        out_shape=jax.ShapeDtypeStruct((m, dim), BF16),
        scratch_shapes=[pltpu.VMEM((PAIR, PAIR), F32), pltpu.VMEM((tt, PAIR), F32)],
        compiler_params=_params(("parallel", "parallel", "arbitrary"), 32),
        name="rwkv_scan",
    )(r, lw, k, v, a, b, g, r_k, ln_w, ln_b)


def _pad_cols(w, width):
    return jnp.pad(w, ((0, 0), (0, width - w.shape[1])))


def _pad_rows(w, height):
    return jnp.pad(w, ((0, height - w.shape[0]), (0, 0)))


def _dup_kv_heads(w):
    d, c = w.shape
    return jnp.tile(w.reshape(d, c // HEAD_DIM, 1, HEAD_DIM), (1, 1, 2, 1)).reshape(d, 2 * c)


def even_mixer(h, xn, batch, seq, w_in, mix, w0, w2, a0, a2, g2, k_k, k_a, r_k, ln_w, ln_b, sink, w_out, layer,
               dim, swa_heads, swa_kv_heads, tm, tn):
    d_model = xn.shape[1]
    decay_lora, aaa_lora = w2.shape[0], a2.shape[0]
    o = 3 * dim
    o_a, o_g = o + decay_lora, o + decay_lora + aaa_lora
    rwkv_cols = o_g + g2.shape[0]
    scale = HEAD_DIM ** -0.5

    def rwkv_layout(t):
        return jnp.concatenate([t[..., :o], _pad_cols(t[..., o:o_a], LORA_PAD), _pad_cols(t[..., o_a:o_g], LORA_PAD),
                                t[..., o_g:rwkv_cols]], axis=-1)

    w_a = rwkv_layout(w_in).astype(BF16)
    q_end = rwkv_cols + swa_heads * HEAD_DIM
    k_end = q_end + swa_kv_heads * HEAD_DIM
    w_b = jnp.concatenate([w_in[:, rwkv_cols:q_end] * scale, _dup_kv_heads(w_in[:, q_end:k_end]),
                           _dup_kv_heads(w_in[:, k_end:])], axis=1).astype(BF16)

    pa = matmul_sum([(xn, d_model, 0)], [(w_a, d_model, 0)], w_a.shape[1], F32, tm, tn)
    pb = matmul_sum([(xn, d_model, 0)], [(w_b, d_model, 0)], w_b.shape[1], BF16, tm, tn)

    row = lambda t: t.reshape(1, -1)
    prep = rwkv_prep(pa, rwkv_layout(row(mix)), row(w0), _pad_rows(w2, LORA_PAD), row(a0), _pad_rows(a2, LORA_PAD),
                     g2, row(k_k), row(k_a), batch, seq, dim)
    y_a = rwkv_scan(*prep, row(r_k), row(ln_w), row(ln_b), batch, seq)

    q_pairs, kv_pairs = swa_heads // 2, swa_kv_heads
    kv_w = kv_pairs * PAIR
    kv_base = (q_pairs * PAIR) // kv_w
    sink_lanes = jnp.repeat(sink, HEAD_DIM).reshape(1, -1)
    y_b = swa_attention(pb, 0, kv_base, kv_base + 1, batch, seq, q_pairs, kv_pairs, SWA_WINDOW - 1, sink_lanes)

    return _out_proj(h, y_a, y_b, w_out, layer, tm, tn)


def odd_mixer(h, xn, batch, seq, w_in, b_f, w_out, layer, fox_heads, dil_heads, tm, tn, fox_tq):
    d_model = xn.shape[1]
    cw, dw = fox_heads * HEAD_DIM, dil_heads * HEAD_DIM
    scale = HEAD_DIM ** -0.5
    f0 = 3 * cw
    d0 = f0 + fox_heads
    w_c = jnp.concatenate([w_in[:, :cw] * scale, w_in[:, cw:f0]], axis=1).astype(BF16)
    w_d = jnp.concatenate([w_in[:, d0:d0 + dw] * scale, w_in[:, d0 + dw:]], axis=1).astype(BF16)
    w_f = _pad_cols(w_in[:, f0:d0], PAIR).astype(BF16)
    proj = matmul_sum([(xn, d_model, 0)], [(w_c, d_model, 0)], f0, BF16, tm, tn)
    proj_d = matmul_sum([(xn, d_model, 0)], [(w_d, d_model, 0)], 3 * dw, F32, tm, tn)
    fc = matmul_sum([(xn, d_model, 0)], [(w_f, d_model, 0)], PAIR, F32, tm, PAIR)

    fox_pairs = fox_heads // 2
    c, key_bias = forget_cumsum(fc, _pad_cols(b_f.reshape(1, -1), PAIR), batch, seq, fox_pairs)
    c_col = c[:, :fox_heads].reshape(batch, seq, fox_heads).transpose(0, 2, 1)[..., None]
    y_c = fox_attention(proj, key_bias, c_col, batch, seq, fox_pairs, 0, fox_pairs, 2 * fox_pairs, fox_tq)

    y_d = dilated_attention(proj_d, batch, seq, dil_heads // 2)

    return _out_proj(h, y_c, y_d, w_out, layer, tm, tn)


def _out_proj(h, y_1, y_2, w_out, layer, tm, tn):
    w1, w2 = y_1.shape[1], y_2.shape[1]
    kw = math.gcd(w1, w2)
    n1, n2 = w1 // kw, w2 // kw
    a_ops = [(y_1, kw, c) for c in range(n1)] + [(y_2, kw, c) for c in range(n2)]
    b_ops = [(w_out, kw, c) for c in range(n1 + n2)]
    return matmul_sum(a_ops, b_ops, w_out.shape[2], F32, tm, tn, residual=h, layer=layer)


def ffn(h, xn, w_gate, w_up, w_down, layer, tm, tn_hidden, tm_down, tn_down):
    half = w_gate.shape[2] // 2
    hidden = swiglu_hidden(xn, w_gate, w_up, layer, tm, tn_hidden)
    return matmul_sum([(hidden, half, 0), (hidden, half, 1)], [(w_down, half, 0), (w_down, half, 1)],
                      w_down.shape[2], F32, tm_down, tn_down, residual=h, layer=layer, vmem_mib=56)


def kernel(x, norm_mix, norm_ffn, norm_final, ffn_w_gate, ffn_w_up, ffn_w_down, ev_w_in, rwkv_mix, rwkv_w0, rwkv_w2, rwkv_a0, rwkv_a2, rwkv_g2, rwkv_k_k, rwkv_k_a, rwkv_r_k, rwkv_ln_w, rwkv_ln_b, swa_sink, ev_w_out, od_w_in, fox_b_f, od_w_out):
    batch, seq, d_model = x.shape
    depth = norm_mix.shape[0]
    dim = rwkv_w0.shape[1]
    swa_heads = swa_sink.shape[1]
    fox_heads = fox_b_f.shape[1]
    dil_heads = d_model // HEAD_DIM - fox_heads
    tm, tn = 1024, 512
    w_gate, w_up, w_down = ffn_w_gate.astype(BF16), ffn_w_up.astype(BF16), ffn_w_down.astype(BF16)
    w_out_even, w_out_odd = ev_w_out.astype(BF16), od_w_out.astype(BF16)
    h = x.reshape(batch * seq, d_model)
    for layer in range(depth):
        i = layer // 2
        xn = rmsnorm(h, norm_mix[layer], BF16)
        if layer % 2 == 0:
            h = even_mixer(h, xn, batch, seq, ev_w_in[i], rwkv_mix[i], rwkv_w0[i], rwkv_w2[i], rwkv_a0[i], rwkv_a2[i],
                           rwkv_g2[i], rwkv_k_k[i], rwkv_k_a[i], rwkv_r_k[i].reshape(-1), rwkv_ln_w[i], rwkv_ln_b[i],
                           swa_sink[i], w_out_even, i, dim, swa_heads, swa_heads // 8, tm, tn)
        else:
            h = odd_mixer(h, xn, batch, seq, od_w_in[i], fox_b_f[i], w_out_odd, i, fox_heads, dil_heads, tm, tn, 512)
        xn = rmsnorm(h, norm_ffn[layer], BF16)
        h = ffn(h, xn, w_gate, w_up, w_down, layer, tm, 256, 512, 256)
    return rmsnorm(h, norm_final, F32).reshape(batch, seq, d_model)
```

```python
import functools
import math

import jax
import jax.numpy as jnp
from jax import lax
from jax.experimental import pallas as pl
from jax.experimental.pallas import tpu as pltpu

F32 = jnp.float32
BF16 = jnp.bfloat16
HIGHEST = lax.Precision.HIGHEST

HEAD_DIM = 64
PAIR = 2 * HEAD_DIM
BLOCK = 128
SWA_WINDOW = 128
DILATED_PAIRS = ((128, 1), (512, 4), (2048, 16))
RMS_EPS = 1e-6
RWKV_LN_EPS = 64e-5
NEG_INF = -1e30
DILATED_GROUP = 8
RWKV_CHUNK = 64
LORA_PAD = 128
MIB = 1024 * 1024


def _params(semantics, vmem_mib):
    return pltpu.CompilerParams(dimension_semantics=semantics, vmem_limit_bytes=vmem_mib * MIB)


def _dot(a, b, precision=None):
    return jnp.dot(a, b, preferred_element_type=F32, precision=precision)


def _dot_bf16(a, b):
    return jnp.dot(a.astype(BF16), b.astype(BF16), preferred_element_type=F32)


def _dot_nt(a, b, precision=None):
    return lax.dot_general(a, b, (((1,), (1,)), ((), ())), preferred_element_type=F32, precision=precision)


def _split3(x):
    hi = x.astype(BF16)
    rem = x - hi.astype(F32)
    mid = rem.astype(BF16)
    lo = (rem - mid.astype(F32)).astype(BF16)
    return hi, mid, lo


def _dot_exact_lhs(m, x):
    hi, mid, lo = _split3(x)
    return _dot(m, lo) + _dot(m, mid) + _dot(m, hi)


def _dot_exact_rhs(x, m):
    hi, mid, lo = _split3(x)
    return _dot(lo, m) + _dot(mid, m) + _dot(hi, m)


def _dot_x3(a, b):
    a_hi, b_hi = a.astype(BF16), b.astype(BF16)
    a_lo, b_lo = (a - a_hi.astype(F32)).astype(BF16), (b - b_hi.astype(F32)).astype(BF16)
    return _dot(a_lo, b_hi) + _dot(a_hi, b_lo) + _dot(a_hi, b_hi)


def _sigmoid(x):
    return 1.0 / (1.0 + jnp.exp(-x))


def _softplus(x):
    return jnp.maximum(x, 0.0) + jnp.log(1.0 + jnp.exp(-jnp.abs(x)))


def _tile(n, target):
    t = min(n, target) // PAIR * PAIR
    while n % t:
        t -= PAIR
    return t


def _head_sum_matrix():
    r = lax.broadcasted_iota(jnp.int32, (PAIR, PAIR), 0) // HEAD_DIM
    c = lax.broadcasted_iota(jnp.int32, (PAIR, PAIR), 1) // HEAD_DIM
    return (r == c).astype(F32)


def _rmsnorm_kernel(x_ref, g_ref, o_ref):
    x = x_ref[...]
    ms = jnp.mean(x * x, axis=-1, keepdims=True)
    o_ref[...] = (x * lax.rsqrt(ms + RMS_EPS) * g_ref[...]).astype(o_ref.dtype)


def rmsnorm(x, g, out_dtype, tm=256):
    m, d = x.shape
    return pl.pallas_call(
        _rmsnorm_kernel,
        grid=(m // tm,),
        in_specs=[pl.BlockSpec((tm, d), lambda i: (i, 0)), pl.BlockSpec((1, d), lambda i: (0, 0))],
        out_specs=pl.BlockSpec((tm, d), lambda i: (i, 0)),
        out_shape=jax.ShapeDtypeStruct((m, d), out_dtype),
        compiler_params=_params(("parallel",), 40),
        name="rmsnorm",
    )(x, g.reshape(1, d))


def _matmul_sum_kernel(*refs, n_ops, has_res):
    a_refs, b_refs = refs[:n_ops], refs[n_ops:2 * n_ops]
    o_ref = refs[-1]
    acc = _dot(a_refs[0][...], b_refs[0][...])
    for a_ref, b_ref in zip(a_refs[1:], b_refs[1:]):
        acc = acc + _dot(a_ref[...], b_ref[...])
    if has_res:
        acc = acc + refs[2 * n_ops][...]
    o_ref[...] = acc.astype(o_ref.dtype)


def matmul_sum(a_ops, b_ops, n_out, out_dtype, tm, tn, residual=None, layer=None, a_buffers=2, vmem_mib=48):
    m = a_ops[0][0].shape[0]
    tm, tn = _tile(m, tm), _tile(n_out, tn)
    in_specs, args = [], []
    a_mode = {} if a_buffers == 2 else {"pipeline_mode": pl.Buffered(a_buffers)}
    for arr, kw, kb in a_ops:
        in_specs.append(pl.BlockSpec((tm, kw), lambda i, j, kb=kb: (i, kb), **a_mode))
        args.append(arr)
    for arr, kw, kb in b_ops:
        if layer is None:
            in_specs.append(pl.BlockSpec((kw, tn), lambda i, j, kb=kb: (kb, j)))
        else:
            in_specs.append(pl.BlockSpec((None, kw, tn), lambda i, j, kb=kb: (layer, kb, j)))
        args.append(arr)
    if residual is not None:
        in_specs.append(pl.BlockSpec((tm, tn), lambda i, j: (i, j)))
        args.append(residual)
    return pl.pallas_call(
        functools.partial(_matmul_sum_kernel, n_ops=len(a_ops), has_res=residual is not None),
        grid=(m // tm, n_out // tn),
        in_specs=in_specs,
        out_specs=pl.BlockSpec((tm, tn), lambda i, j: (i, j)),
        out_shape=jax.ShapeDtypeStruct((m, n_out), out_dtype),
        compiler_params=_params(("parallel", "parallel"), vmem_mib),
        name="matmul_sum",
    )(*args)


def _swiglu_kernel(x_ref, wg_ref, wu_ref, o_ref):
    x = x_ref[...]
    gate = _dot(x, wg_ref[...])
    up = _dot(x, wu_ref[...])
    o_ref[...] = (gate * _sigmoid(gate) * up).astype(o_ref.dtype)


def swiglu_hidden(x, wg, wu, layer, tm, tn):
    m, k = x.shape
    f = wg.shape[2]
    tm, tn = _tile(m, tm), _tile(f, tn)
    w_spec = pl.BlockSpec((None, k, tn), lambda i, j: (layer, 0, j))
    return pl.pallas_call(
        _swiglu_kernel,
        grid=(m // tm, f // tn),
        in_specs=[pl.BlockSpec((tm, k), lambda i, j: (i, 0)), w_spec, w_spec],
        out_specs=pl.BlockSpec((tm, tn), lambda i, j: (i, j)),
        out_shape=jax.ShapeDtypeStruct((m, f), BF16),
        compiler_params=_params(("parallel", "parallel"), 48),
        name="swiglu_hidden",
    )(x, wg, wu)


def _band_mask(max_dist, has_prev):
    row = lax.broadcasted_iota(jnp.int32, (BLOCK, 2 * BLOCK), 0)
    col = lax.broadcasted_iota(jnp.int32, (BLOCK, 2 * BLOCK), 1)
    dist = row + BLOCK - col
    return (dist >= 0) & (dist <= max_dist) & ((col >= BLOCK) | has_prev)


def _pair_attention_group(qs, ks, vs, mask, head0, sinks=None):
    g = len(qs)
    masked_q = [jnp.where(head0 if e == 0 else jnp.logical_not(head0), q, jnp.zeros_like(q))
                for q in qs for e in range(2)]
    if len(ks) == 1:
        s = _dot_nt(jnp.concatenate(masked_q, axis=0), ks[0]).reshape(2 * g, BLOCK, 2 * BLOCK)
    else:
        s = jnp.stack([_dot_nt(q, ks[u // 2]) for u, q in enumerate(masked_q)])
    s = jnp.where(mask[None], s, NEG_INF)
    mx = jnp.max(s, axis=-1, keepdims=True)
    if sinks is not None:
        sk = jnp.stack(sinks)
        mx = jnp.maximum(mx, sk)
    pe = jnp.exp(s - mx)
    den = jnp.sum(pe, axis=-1, keepdims=True)
    if sinks is not None:
        den = den + jnp.exp(sk - mx)
    probs = pe.astype(BF16)
    inv_den = 1.0 / den
    lse = mx + jnp.log(den)

    def stack_heads(v):
        return jnp.concatenate([jnp.where(head0, v, jnp.zeros_like(v)), jnp.where(head0, jnp.zeros_like(v), v)], axis=0)

    p_cat = [jnp.concatenate([probs[2 * u], probs[2 * u + 1]], axis=1) for u in range(g)]
    if len(vs) == 1:
        o_all = _dot(jnp.concatenate(p_cat, axis=0), stack_heads(vs[0]))
        outs = [o_all[u * BLOCK:(u + 1) * BLOCK] for u in range(g)]
    else:
        outs = [_dot(p, stack_heads(v)) for p, v in zip(p_cat, vs)]
    return [(o * jnp.where(head0, inv_den[2 * u], inv_den[2 * u + 1]), jnp.where(head0, lse[2 * u], lse[2 * u + 1]))
            for u, o in enumerate(outs)]


def _swa_kernel(q_ref, kc_ref, kp_ref, vc_ref, vp_ref, sink_ref, o_ref, *, n_pairs, rep, max_dist):
    mask = _band_mask(max_dist, pl.program_id(1) > 0)
    head0 = lax.broadcasted_iota(jnp.int32, (1, PAIR), 1) < HEAD_DIM
    for kv in range(n_pairs // rep):
        kl = pl.ds(kv * PAIR, PAIR)
        k = jnp.concatenate([kp_ref[:, kl], kc_ref[:, kl]], axis=0)
        v = jnp.concatenate([vp_ref[:, kl], vc_ref[:, kl]], axis=0)
        pairs = range(kv * rep, (kv + 1) * rep)
        sinks = [sink_ref[0:1, pl.ds(p * PAIR + e * HEAD_DIM, 1)] for p in pairs for e in range(2)]
        outs = _pair_attention_group([q_ref[:, pl.ds(p * PAIR, PAIR)] for p in pairs], [k], [v], mask, head0, sinks)
        for p, (o, _) in zip(pairs, outs):
            o_ref[:, pl.ds(p * PAIR, PAIR)] = o.astype(o_ref.dtype)


def swa_attention(qkv, q_col, k_col, v_col, batch, seq, n_pairs, n_kv_pairs, max_dist, sink):
    nb = seq // BLOCK
    qw, kw = n_pairs * PAIR, n_kv_pairs * PAIR
    cur = lambda c: (lambda b, i: (b * nb + i, c))
    prev = lambda c: (lambda b, i: (b * nb + jnp.maximum(i - 1, 0), c))
    return pl.pallas_call(
        functools.partial(_swa_kernel, n_pairs=n_pairs, rep=n_pairs // n_kv_pairs, max_dist=max_dist),
        grid=(batch, nb),
        in_specs=[pl.BlockSpec((BLOCK, qw), cur(q_col)),
                  pl.BlockSpec((BLOCK, kw), cur(k_col)), pl.BlockSpec((BLOCK, kw), prev(k_col)),
                  pl.BlockSpec((BLOCK, kw), cur(v_col)), pl.BlockSpec((BLOCK, kw), prev(v_col)),
                  pl.BlockSpec((1, qw), lambda b, i: (0, 0))],
        out_specs=pl.BlockSpec((BLOCK, qw), cur(0)),
        out_shape=jax.ShapeDtypeStruct((batch * seq, qw), BF16),
        compiler_params=_params(("parallel", "parallel"), 32),
        name="swa_attention",
    )(qkv, qkv, qkv, qkv, qkv, sink)


def _dilated_kernel(q_ref, kc_ref, kp_ref, vc_ref, vp_ref, y_ref, o_ref, lse_ref):
    slab = q_ref.shape[0]
    not_first = pl.program_id(1) > 0
    head0 = lax.broadcasted_iota(jnp.int32, (1, PAIR), 1) < HEAD_DIM
    for branch, (window, dil) in enumerate(DILATED_PAIRS):
        span = BLOCK * dil
        inner_mask = _band_mask(window // dil, True)
        edge_mask = _band_mask(window // dil, not_first)
        units = [(blk, p) for blk in range(slab // span) for p in range(dil)]
        edge_units = [u for u in units if u[0] == 0]
        inner_units = [u for u in units if u[0] > 0]
        groups = [(edge_units[g:g + DILATED_GROUP], edge_mask) for g in range(0, len(edge_units), DILATED_GROUP)]
        groups += [(inner_units[g:g + DILATED_GROUP], inner_mask) for g in range(0, len(inner_units), DILATED_GROUP)]
        for group, mask in groups:
            qs, ks, vs, rows = [], [], [], []
            for blk, p in group:
                cur = pl.ds(blk * span + p, BLOCK, stride=dil)
                if blk:
                    before = pl.ds((blk - 1) * span + p, BLOCK, stride=dil)
                    k_prev, v_prev = kc_ref[before, :], vc_ref[before, :]
                else:
                    before = pl.ds(slab - span + p, BLOCK, stride=dil)
                    k_prev, v_prev = kp_ref[before, :], vp_ref[before, :]
                qs.append(q_ref[cur, :].astype(BF16))
                ks.append(jnp.concatenate([k_prev, kc_ref[cur, :]], axis=0).astype(BF16))
                vs.append(jnp.concatenate([v_prev, vc_ref[cur, :]], axis=0).astype(BF16))
                rows.append(cur)
            for cur, (o, lse) in zip(rows, _pair_attention_group(qs, ks, vs, mask, head0)):
                o_ref[branch, cur, :] = o
                lse_ref[branch, cur, :] = lse
    l1, l2, l3 = lse_ref[0], lse_ref[1], lse_ref[2]
    mx = jnp.maximum(jnp.maximum(l1, l2), l3)
    e1, e2, e3 = jnp.exp(l1 - mx), jnp.exp(l2 - mx), jnp.exp(l3 - mx)
    num = e1 * o_ref[0] + e2 * o_ref[1] + e3 * o_ref[2]
    y_ref[...] = (num / (e1 + e2 + e3)).astype(y_ref.dtype)


def dilated_attention(qkv, batch, seq, n_pairs):
    slab = BLOCK * max(dil for _, dil in DILATED_PAIRS)
    ns = seq // slab
    cur = lambda base: (lambda b, s, p: (b * ns + s, base + p))
    prev = lambda base: (lambda b, s, p: (b * ns + jnp.maximum(s - 1, 0), base + p))
    block = lambda index_map: pl.BlockSpec((slab, PAIR), index_map)
    n_branches = len(DILATED_PAIRS)
    return pl.pallas_call(
        _dilated_kernel,
        grid=(batch, ns, n_pairs),
        in_specs=[block(cur(0)), block(cur(n_pairs)), block(prev(n_pairs)),
                  block(cur(2 * n_pairs)), block(prev(2 * n_pairs))],
        out_specs=block(cur(0)),
        out_shape=jax.ShapeDtypeStruct((batch * seq, n_pairs * PAIR), BF16),
        scratch_shapes=[pltpu.VMEM((n_branches, slab, PAIR), F32), pltpu.VMEM((n_branches, slab, PAIR), F32)],
        compiler_params=_params(("parallel", "parallel", "parallel"), 40),
        name="dilated_attention",
    )(qkv, qkv, qkv, qkv, qkv)


def _forget_cumsum_kernel(f_ref, b_ref, c_ref, kb_ref, carry_ref):
    @pl.when(pl.program_id(1) == 0)
    def _():
        carry_ref[...] = jnp.zeros_like(carry_ref)

    blk = f_ref.shape[0]
    log_f = -_softplus(-(f_ref[...] + b_ref[...]))
    tri = (lax.broadcasted_iota(jnp.int32, (blk, blk), 0) >= lax.broadcasted_iota(jnp.int32, (blk, blk), 1))
    c = _dot_exact_lhs(tri.astype(BF16), log_f) + carry_ref[...]
    c_ref[...] = c
    carry_ref[...] = c[blk - 1:blk, :]

    width = kb_ref.shape[1]
    head = lax.broadcasted_iota(jnp.int32, (c.shape[1], width), 0)
    out = lax.broadcasted_iota(jnp.int32, (c.shape[1], width), 1)
    pair, lane = out // PAIR, out % PAIR
    placed = None
    for piece, part in enumerate(_split3(-c)):
        select = ((head == 2 * pair) & (lane == piece)) | ((head == 2 * pair + 1) & (lane == 3 + piece))
        term = _dot(part, select.astype(BF16))
        placed = term if placed is None else placed + term
    kb_ref[...] = placed.astype(BF16)


def forget_cumsum(fc, b_f, batch, seq, n_pairs, blk=512):
    w = fc.shape[1]
    nb = seq // blk
    row_block = lambda width: pl.BlockSpec((blk, width), lambda b, i: (b * nb + i, 0))
    return pl.pallas_call(
        _forget_cumsum_kernel,
        grid=(batch, nb),
        in_specs=[row_block(w), pl.BlockSpec((1, w), lambda b, i: (0, 0))],
        out_specs=[row_block(w), row_block(n_pairs * PAIR)],
        out_shape=[jax.ShapeDtypeStruct(fc.shape, F32), jax.ShapeDtypeStruct((fc.shape[0], n_pairs * PAIR), BF16)],
        scratch_shapes=[pltpu.VMEM((1, w), F32)],
        compiler_params=_params(("parallel", "arbitrary"), 16),
        name="forget_cumsum",
    )(fc, b_f)


FOX_ROWS = 32
FOX_PAIRS = 4


def _fox_kernel(qi_ref, kj_ref, q_ref, k_ref, v_ref, cb_ref, cq_ref, o_ref,
                m_ref, al_ref, cqr_ref, acc_ref, s_ref, p_ref):
    t = pl.program_id(2)
    i, j = qi_ref[t], kj_ref[t]
    tq, tk = q_ref.shape[0], k_ref.shape[0]
    n_tiles = tk // PAIR
    heads = range(2 * FOX_PAIRS)
    lane1 = lax.broadcasted_iota(jnp.int32, (1, PAIR), 1)
    head0 = lane1 < HEAD_DIM

    @pl.when(j == 0)
    def _():
        m_ref[...] = jnp.full_like(m_ref, NEG_INF)
        acc_ref[...] = jnp.zeros_like(acc_ref)
        for h in heads:
            cqr_ref[h] = jnp.broadcast_to(cq_ref[0, h], (tq, PAIR))

    def step(masked):
        lane = lax.broadcasted_iota(jnp.int32, (FOX_ROWS, PAIR), 1)
        sub = lax.broadcasted_iota(jnp.int32, (FOX_ROWS, PAIR), 0)
        for h in heads:
            g, e = divmod(h, 2)
            pair = pl.ds(g * PAIR, PAIR)
            q = q_ref[:, pair] * (HEAD_DIM ** -0.5)
            sel = head0 if e == 0 else jnp.logical_not(head0)
            bias_on = jnp.broadcast_to(((lane1 >= 3 * e) & (lane1 < 3 * e + 3)).astype(BF16), (tq, PAIR))
            s_ref[h] = _dot_nt(jnp.concatenate([jnp.where(sel, q, jnp.zeros_like(q)), bias_on], axis=1),
                               jnp.concatenate([k_ref[:, pair], cb_ref[:, pair]], axis=1))
        pv = []
        for h in heads:
            g, e = divmod(h, 2)
            sel = head0 if e == 0 else jnp.logical_not(head0)
            for c in range(tq // FOX_ROWS):
                rows = pl.ds(c * FOX_ROWS, FOX_ROWS)
                tiles = []
                for n in range(n_tiles):
                    s = s_ref[h, rows, pl.ds(n * PAIR, PAIR)]
                    if masked:
                        s = jnp.where(lane + n * PAIR <= sub + c * FOX_ROWS, s, NEG_INF)
                    tiles.append(s)
                mx = tiles[0]
                for s in tiles[1:]:
                    mx = jnp.maximum(mx, s)
                cq = cqr_ref[h, rows, :]
                m_prev = m_ref[h, rows, :]
                m_new = jnp.maximum(m_prev, jnp.max(mx, axis=-1, keepdims=True) + cq)
                shift = cq - m_new
                for n, s in enumerate(tiles):
                    p_ref[rows, pl.ds(h * tk + n * PAIR, PAIR)] = jnp.exp(s + shift).astype(BF16)
                m_ref[h, rows, :] = m_new
                al_ref[h, rows, :] = jnp.exp(m_prev - m_new)
            v = v_ref[:, pl.ds(g * PAIR, PAIR)]
            v_aug = jnp.concatenate([jnp.where(sel, v, jnp.zeros_like(v)),
                                     jnp.broadcast_to((lane1 == e).astype(BF16), (tk, PAIR))], axis=1)
            pv.append(_dot(p_ref[:, pl.ds(h * tk, tk)], v_aug))
        for g in range(FOX_PAIRS):
            a0, a1 = al_ref[2 * g], al_ref[2 * g + 1]
            alpha = jnp.concatenate([jnp.where(head0, a0, a1), jnp.where(lane1 == 0, a0, a1)], axis=1)
            acc = acc_ref[g] * alpha + pv[2 * g] + pv[2 * g + 1]
            if masked:
                den = jnp.where(head0, acc[:, PAIR:PAIR + 1], acc[:, PAIR + 1:PAIR + 2])
                o_ref[:, pl.ds(g * PAIR, PAIR)] = (acc[:, :PAIR] / den).astype(o_ref.dtype)
            else:
                acc_ref[g] = acc

    @pl.when(j < i)
    def _():
        step(False)

    @pl.when(j == i)
    def _():
        step(True)


def fox_attention(qkv, key_bias, c_col, batch, seq, n_pairs, q_col, k_col, v_col, tq):
    nq = seq // tq
    qi = jnp.asarray([i for i in range(nq) for _ in range(i + 1)], jnp.int32)
    kj = jnp.asarray([j for i in range(nq) for j in range(i + 1)], jnp.int32)
    gw, heads = FOX_PAIRS * PAIR, 2 * FOX_PAIRS
    rep = pltpu.VMEM((heads, tq, PAIR), F32)
    grid_spec = pltpu.PrefetchScalarGridSpec(
        num_scalar_prefetch=2,
        grid=(batch, n_pairs // FOX_PAIRS, qi.shape[0]),
        in_specs=[pl.BlockSpec((tq, gw), lambda b, p, t, qi, kj: (b * nq + qi[t], q_col // FOX_PAIRS + p)),
                  pl.BlockSpec((tq, gw), lambda b, p, t, qi, kj: (b * nq + kj[t], k_col // FOX_PAIRS + p)),
                  pl.BlockSpec((tq, gw), lambda b, p, t, qi, kj: (b * nq + kj[t], v_col // FOX_PAIRS + p)),
                  pl.BlockSpec((tq, gw), lambda b, p, t, qi, kj: (b * nq + kj[t], p)),
                  pl.BlockSpec((1, heads, tq, 1), lambda b, p, t, qi, kj: (b, p, qi[t], 0))],
        out_specs=pl.BlockSpec((tq, gw), lambda b, p, t, qi, kj: (b * nq + qi[t], p)),
        scratch_shapes=[rep, rep, rep, pltpu.VMEM((FOX_PAIRS, tq, 2 * PAIR), F32),
                        pltpu.VMEM((heads, tq, tq), F32), pltpu.VMEM((tq, heads * tq), BF16)])
    return pl.pallas_call(
        _fox_kernel,
        grid_spec=grid_spec,
        out_shape=jax.ShapeDtypeStruct((batch * seq, n_pairs * PAIR), BF16),
        compiler_params=_params(("parallel", "parallel", "arbitrary"), 32),
        name="fox_attention",
    )(qi, kj, qkv, qkv, qkv, key_bias, c_col)


def _rwkv_prep_kernel(x_ref, xp_ref, xl_ref, xlp_ref, mix_ref, mixl_ref, w0_ref, w2_ref, a0_ref, a2_ref, g2_ref,
                      kk_ref, ka_ref, r_ref, lw_ref, k_ref, v_ref, a_ref, b_ref, g_ref, *, dim):
    first = pl.program_id(1) == 0
    row = lax.broadcasted_iota(jnp.int32, (x_ref.shape[0], 1), 0)

    def shift_lerp(cur_ref, prev_ref, mix):
        x = cur_ref[...]
        prev_row = jnp.where(first, 0.0, prev_ref[7:8, :])
        shifted = jnp.where(row == 0, prev_row, pltpu.roll(x, 1, 0))
        return x + (shifted - x) * mix

    x = shift_lerp(x_ref, xp_ref, mix_ref[...])
    xl = shift_lerp(xl_ref, xlp_ref, mixl_ref[...])
    r, k, v = x[:, 0:dim], x[:, dim:2 * dim], x[:, 2 * dim:3 * dim]
    xw, xa, xg = xl[:, 0:LORA_PAD], xl[:, LORA_PAD:2 * LORA_PAD], xl[:, 2 * LORA_PAD:]
    w_log = -_softplus(-(w0_ref[...] + _dot_x3(jnp.tanh(xw), w2_ref[...]))) - 0.5
    a = _sigmoid(a0_ref[...] + _dot_x3(xa, a2_ref[...]))
    g = _dot_x3(_sigmoid(xg), g2_ref[...])

    kk = k * kk_ref[...]
    ones = _head_sum_matrix().astype(BF16)
    norm = []
    for p in range(dim // PAIR):
        kp = kk[:, p * PAIR:(p + 1) * PAIR]
        norm.append(jnp.sqrt(_dot_exact_rhs(kp * kp, ones)))
    kk = kk / jnp.maximum(jnp.concatenate(norm, axis=1), 1e-12)

    r_ref[...] = r.astype(r_ref.dtype)
    lw_ref[...] = -jnp.exp(w_log)
    k_ref[...] = (k * (1.0 + (a - 1.0) * ka_ref[...])).astype(k_ref.dtype)
    v_ref[...] = v.astype(v_ref.dtype)
    a_ref[...] = (-kk).astype(a_ref.dtype)
    b_ref[...] = (kk * a).astype(b_ref.dtype)
    g_ref[...] = g.astype(g_ref.dtype)


def rwkv_prep(pa, pl_, mix, mix_l, w0, w2, a0, a2, g2, k_k, k_a, batch, seq, dim, tt=128):
    m = pa.shape[0]
    nt = seq // tt
    full = lambda arr: pl.BlockSpec(arr.shape, lambda b, i: (0, 0))
    cur = lambda arr: pl.BlockSpec((tt, arr.shape[1]), lambda b, i: (b * nt + i, 0))
    before = lambda arr: pl.BlockSpec((8, arr.shape[1]),
                                      lambda b, i: (jnp.maximum((b * nt + i) * (tt // 8) - 1, 0), 0))
    out_spec = pl.BlockSpec((tt, dim), lambda b, i: (b * nt + i, 0))
    consts = [mix, mix_l, w0, w2, a0, a2, g2, k_k, k_a]
    return pl.pallas_call(
        functools.partial(_rwkv_prep_kernel, dim=dim),
        grid=(batch, nt),
        in_specs=[cur(pa), before(pa), cur(pl_), before(pl_)] + [full(c) for c in consts],
        out_specs=[out_spec] * 7,
        out_shape=[jax.ShapeDtypeStruct((m, dim), F32 if name == "lw" else BF16)
                   for name in ("r", "lw", "k", "v", "a", "b", "g")],
        compiler_params=_params(("parallel", "parallel"), 48),
        name="rwkv_prep",
    )(pa, pa, pl_, pl_, *consts)


def _stack_heads(x, head0):
    zero = jnp.zeros_like(x)
    return jnp.concatenate([jnp.where(head0, x, zero), jnp.where(head0, zero, x)], axis=0)


def _rwkv_scan_kernel(r_ref, lw_ref, k_ref, v_ref, a_ref, b_ref, g_ref, rk_ref, lnw_ref, lnb_ref,
                      o_ref, s_ref, y_ref):
    c_len = RWKV_CHUNK
    tt = r_ref.shape[0]
    chunks = range(tt // c_len)

    @pl.when(pl.program_id(2) == 0)
    def _():
        s_ref[...] = jnp.zeros_like(s_ref)

    head0 = lax.broadcasted_iota(jnp.int32, (1, PAIR), 1) < HEAD_DIM
    n2 = 2 * c_len
    row = lax.broadcasted_iota(jnp.int32, (n2, n2), 0)
    col = lax.broadcasted_iota(jnp.int32, (n2, n2), 1)
    t_loc, j_loc = row % c_len, col % c_len
    same_head = (row // c_len) == (col // c_len)
    strict = same_head & (j_loc < t_loc)
    incl = same_head & (j_loc <= t_loc)
    incl2 = jnp.concatenate([incl, incl], axis=1)
    sub_block = same_head & ((t_loc // 16) == (j_loc // 16))
    eye = (row == col).astype(F32)
    zeros = jnp.zeros((n2, PAIR), F32)
    tri = (lax.broadcasted_iota(jnp.int32, (c_len, c_len), 0)
           >= lax.broadcasted_iota(jnp.int32, (c_len, c_len), 1)).astype(BF16)

    def rows(x, c):
        return x[c * c_len:(c + 1) * c_len]

    lw = lw_ref[...]
    r, k, v, a, b = (ref[...].astype(F32) for ref in (r_ref, k_ref, v_ref, a_ref, b_ref))
    cw = jnp.concatenate([_dot_exact_lhs(tri, rows(lw, c)) for c in chunks], axis=0)
    e_pos, e_neg = jnp.exp(cw), jnp.exp(-cw)
    a_t, r_t, b_t, k_t = a * jnp.exp(cw - lw), r * e_pos, b * e_neg, k * e_neg

    e_last = [jnp.exp(rows(cw, c)[c_len - 1:c_len, :]) for c in chunks]
    a_s = [_stack_heads(rows(a_t, c), head0) for c in chunks]
    r_s = [_stack_heads(rows(r_t, c), head0) for c in chunks]
    v_s = [_stack_heads(rows(v, c), head0) for c in chunks]
    scores = [_dot_nt(jnp.concatenate([a_s[c], r_s[c]], axis=0).astype(BF16),
                      jnp.concatenate([_stack_heads(rows(b_t, c), head0),
                                       _stack_heads(rows(k_t, c), head0)], axis=0).astype(BF16)) for c in chunks]
    l_ab = [jnp.where(strict, s[:n2, :n2], 0.0) for s in scores]

    d1 = [jnp.where(sub_block, x, 0.0) for x in l_ab]
    d2 = [_dot_bf16(x, x) for x in d1]
    d4 = [_dot_bf16(x, x) for x in d2]
    d8 = [_dot_bf16(x, x) for x in d4]
    t_lo = [_dot_bf16(eye + x, eye + y) for x, y in zip(d1, d2)]
    t_hi = [_dot_bf16(eye + x, eye + y) for x, y in zip(d4, d8)]
    t_diag = [_dot_bf16(x, y) for x, y in zip(t_lo, t_hi)]
    n1 = [_dot_bf16(t, x - d) for t, x, d in zip(t_diag, l_ab, d1)]
    n_sq = [_dot_bf16(x, x) for x in n1]
    t_rest = [_dot_bf16(eye + x, eye + y) for x, y in zip(n1, n_sq)]
    t_inv = [_dot_bf16(x, t) for x, t in zip(t_rest, t_diag)]

    lv = [_dot_bf16(jnp.where(strict, s[:n2, n2:], 0.0), vs) for s, vs in zip(scores, v_s)]
    au = [_dot_bf16(t, jnp.concatenate([x, y], axis=1)) for t, x, y in zip(t_inv, a_s, lv)]
    z = []
    for c in chunks:
        e_end = e_last[c] * rows(e_neg, c)
        bend_s = _stack_heads(rows(b, c) * e_end, head0)
        kend_s = _stack_heads(rows(k, c) * e_end, head0)
        lhs = jnp.concatenate([jnp.where(incl2, scores[c][n2:], 0.0),
                               jnp.concatenate([bend_s.T, kend_s.T], axis=1)], axis=0)
        rhs = jnp.concatenate([au[c], jnp.concatenate([zeros, v_s[c]], axis=1)], axis=0)
        z.append(_dot_bf16(lhs, rhs))

    state = s_ref[...]
    for c in chunks:
        r_hat = r_s[c] + z[c][:n2, :PAIR]
        y_hat = z[c][:n2, PAIR:]
        lhs = jnp.concatenate([r_hat[:c_len] + r_hat[c_len:], z[c][n2:, :PAIR] + eye * e_last[c]], axis=0)
        prod = _dot_bf16(lhs, state)
        y_ref[pl.ds(c * c_len, c_len), :] = prod[:c_len] + y_hat[:c_len] + y_hat[c_len:]
        state = prod[c_len:] + z[c][n2:, PAIR:]
    s_ref[...] = state

    ones = _head_sum_matrix().astype(BF16)
    y = y_ref[...]
    mu = _dot_exact_rhs(y, ones) * (1.0 / HEAD_DIM)
    d = y - mu
    var = _dot_exact_rhs(d * d, ones) * (1.0 / HEAD_DIM)
    yn = d * lax.rsqrt(var + RWKV_LN_EPS) * lnw_ref[...] + lnb_ref[...]
    bonus = _dot_exact_rhs(r * k * rk_ref[...], ones) * v
    o_ref[...] = ((yn + bonus) * g_ref[...]).astype(o_ref.dtype)


def rwkv_scan(r, lw, k, v, a, b, g, r_k, ln_w, ln_b, batch, seq, tt=512):
    m, dim = r.shape
    n_pairs = dim // PAIR
    nt = seq // tt
    seq_spec = pl.BlockSpec((tt, PAIR), lambda bb, p, i: (bb * nt + i, p))
    vec_spec = pl.BlockSpec((1, PAIR), lambda bb, p, i: (0, p))
    return pl.pallas_call(
        _rwkv_scan_kernel,
        grid=(batch, n_pairs, nt),
        in_specs=[seq_spec] * 7 + [vec_spec] * 3,
        out_specs=seq_spec,
        out_shape=jax.ShapeDtypeStruct((m, dim), BF16),
        scratch_shapes=[pltpu.VMEM((PAIR, PAIR), F32), pltpu.VMEM((tt, PAIR), F32)],
        compiler_params=_params(("parallel", "parallel", "arbitrary"), 32),
        name="rwkv_scan",
    )(r, lw, k, v, a, b, g, r_k, ln_w, ln_b)


def _pad_cols(w, width):
    return jnp.pad(w, ((0, 0), (0, width - w.shape[1])))


def _pad_rows(w, height):
    return jnp.pad(w, ((0, height - w.shape[0]), (0, 0)))


def _dup_kv_heads(w):
    d, c = w.shape
    return jnp.tile(w.reshape(d, c // HEAD_DIM, 1, HEAD_DIM), (1, 1, 2, 1)).reshape(d, 2 * c)


def even_mixer(h, xn, batch, seq, w_in, mix, w0, w2, a0, a2, g2, k_k, k_a, r_k, ln_w, ln_b, sink, w_out, layer,
               dim, swa_heads, swa_kv_heads, tm, tn):
    d_model = xn.shape[1]
    decay_lora, aaa_lora = w2.shape[0], a2.shape[0]
    o = 3 * dim
    o_a, o_g = o + decay_lora, o + decay_lora + aaa_lora
    rwkv_cols = o_g + g2.shape[0]
    scale = HEAD_DIM ** -0.5

    def lora_layout(t):
        return jnp.concatenate([_pad_cols(t[..., o:o_a], LORA_PAD), _pad_cols(t[..., o_a:o_g], LORA_PAD),
                                t[..., o_g:rwkv_cols]], axis=-1)

    w_layer = w_in[layer]
    w_l = lora_layout(w_layer)
    q_end = rwkv_cols + swa_heads * HEAD_DIM
    k_end = q_end + swa_kv_heads * HEAD_DIM
    w_b = jnp.concatenate([w_layer[:, rwkv_cols:q_end] * scale, _dup_kv_heads(w_layer[:, q_end:k_end]),
                           _dup_kv_heads(w_layer[:, k_end:])], axis=1)

    pa = matmul_sum([(xn, d_model, 0)], [(w_in, d_model, 0)], o, F32, tm, tn, layer=layer)
    pa_l = matmul_sum([(xn, d_model, 0)], [(w_l, d_model, 0)], w_l.shape[1], F32, tm, tn)
    pb = matmul_sum([(xn, d_model, 0)], [(w_b, d_model, 0)], w_b.shape[1], BF16, tm, tn)

    row = lambda t: t.reshape(1, -1)
    prep = rwkv_prep(pa, pa_l, row(mix)[:, :o], lora_layout(row(mix)), row(w0), _pad_rows(w2, LORA_PAD), row(a0),
                     _pad_rows(a2, LORA_PAD), g2, row(k_k), row(k_a), batch, seq, dim)
    y_a = rwkv_scan(*prep, row(r_k), row(ln_w), row(ln_b), batch, seq)

    q_pairs, kv_pairs = swa_heads // 2, swa_kv_heads
    kv_w = kv_pairs * PAIR
    kv_base = (q_pairs * PAIR) // kv_w
    sink_lanes = jnp.repeat(sink, HEAD_DIM).reshape(1, -1)
    y_b = swa_attention(pb, 0, kv_base, kv_base + 1, batch, seq, q_pairs, kv_pairs, SWA_WINDOW - 1, sink_lanes)

    return _out_proj(h, y_a, y_b, w_out, layer, tm, tn)


def odd_mixer(h, xn, batch, seq, w_in, b_f, w_out, layer, fox_heads, dil_heads, tm, tn, fox_tq):
    d_model = xn.shape[1]
    cw, dw = fox_heads * HEAD_DIM, dil_heads * HEAD_DIM
    scale = HEAD_DIM ** -0.5
    f0 = 3 * cw
    d0 = f0 + fox_heads
    w_layer = w_in[layer]
    w_d = jnp.concatenate([w_layer[:, d0:d0 + dw] * scale, w_layer[:, d0 + dw:]], axis=1)
    w_f = _pad_cols(w_layer[:, f0:d0], PAIR)
    proj = matmul_sum([(xn, d_model, 0)], [(w_in, d_model, 0)], f0, BF16, tm, tn, layer=layer)
    proj_d = matmul_sum([(xn, d_model, 0)], [(w_d, d_model, 0)], 3 * dw, F32, tm, tn)
    fc = matmul_sum([(xn, d_model, 0)], [(w_f, d_model, 0)], PAIR, F32, tm, PAIR)

    fox_pairs = fox_heads // 2
    c, key_bias = forget_cumsum(fc, _pad_cols(b_f.reshape(1, -1), PAIR), batch, seq, fox_pairs)
    c_col = c[:, :fox_heads].reshape(batch, seq, fox_heads).transpose(0, 2, 1)[..., None]
    y_c = fox_attention(proj, key_bias, c_col, batch, seq, fox_pairs, 0, fox_pairs, 2 * fox_pairs, fox_tq)

    y_d = dilated_attention(proj_d, batch, seq, dil_heads // 2)

    return _out_proj(h, y_c, y_d, w_out, layer, tm, tn)


def _out_proj(h, y_1, y_2, w_out, layer, tm, tn):
    w1, w2 = y_1.shape[1], y_2.shape[1]
    kw = math.gcd(w1, w2)
    n1, n2 = w1 // kw, w2 // kw
    a_ops = [(y_1, kw, c) for c in range(n1)] + [(y_2, kw, c) for c in range(n2)]
    b_ops = [(w_out, kw, c) for c in range(n1 + n2)]
    return matmul_sum(a_ops, b_ops, w_out.shape[2], F32, tm, tn, residual=h, layer=layer)


def ffn(h, xn, w_gate, w_up, w_down, layer, tm, tn_hidden, tm_down, tn_down):
    half = w_gate.shape[2] // 2
    hidden = swiglu_hidden(xn, w_gate, w_up, layer, tm, tn_hidden)
    return matmul_sum([(hidden, half, 0), (hidden, half, 1)], [(w_down, half, 0), (w_down, half, 1)],
                      w_down.shape[2], F32, tm_down, tn_down, residual=h, layer=layer, a_buffers=1, vmem_mib=56)


def kernel(x, norm_mix, norm_ffn, norm_final, ffn_w_gate, ffn_w_up, ffn_w_down, ev_w_in, rwkv_mix, rwkv_w0, rwkv_w2, rwkv_a0, rwkv_a2, rwkv_g2, rwkv_k_k, rwkv_k_a, rwkv_r_k, rwkv_ln_w, rwkv_ln_b, swa_sink, ev_w_out, od_w_in, fox_b_f, od_w_out):
    batch, seq, d_model = x.shape
    depth = norm_mix.shape[0]
    dim = rwkv_w0.shape[1]
    swa_heads = swa_sink.shape[1]
    fox_heads = fox_b_f.shape[1]
    dil_heads = d_model // HEAD_DIM - fox_heads
    tm, tn = 1024, 512
    w_gate, w_up, w_down = ffn_w_gate.astype(BF16), ffn_w_up.astype(BF16), ffn_w_down.astype(BF16)
    w_out_even, w_out_odd = ev_w_out.astype(BF16), od_w_out.astype(BF16)
    w_in_even, w_in_odd = ev_w_in.astype(BF16), od_w_in.astype(BF16)
    h = x.reshape(batch * seq, d_model)
    for layer in range(depth):
        i = layer // 2
        xn = rmsnorm(h, norm_mix[layer], BF16)
        if layer % 2 == 0:
            h = even_mixer(h, xn, batch, seq, w_in_even, rwkv_mix[i], rwkv_w0[i], rwkv_w2[i], rwkv_a0[i], rwkv_a2[i],
                           rwkv_g2[i], rwkv_k_k[i], rwkv_k_a[i], rwkv_r_k[i].reshape(-1), rwkv_ln_w[i], rwkv_ln_b[i],
                           swa_sink[i], w_out_even, i, dim, swa_heads, swa_heads // 8, tm, tn)
        else:
            h = odd_mixer(h, xn, batch, seq, w_in_odd, fox_b_f[i], w_out_odd, i, fox_heads, dil_heads, tm, tn, 512)
        xn = rmsnorm(h, norm_ffn[layer], BF16)
        h = ffn(h, xn, w_gate, w_up, w_down, layer, tm, 256, tm, 256)
    return rmsnorm(h, norm_final, F32).reshape(batch, seq, d_model)
```

```python
import functools
import math

import jax
import jax.numpy as jnp
from jax import lax
from jax.experimental import pallas as pl
from jax.experimental.pallas import tpu as pltpu

F32 = jnp.float32
BF16 = jnp.bfloat16
HIGHEST = lax.Precision.HIGHEST

HEAD_DIM = 64
PAIR = 2 * HEAD_DIM
BLOCK = 128
SWA_WINDOW = 128
DILATED_PAIRS = ((128, 1), (512, 4), (2048, 16))
RMS_EPS = 1e-6
RWKV_LN_EPS = 64e-5
NEG_INF = -1e30
DILATED_GROUP = 8
RWKV_CHUNK = 64
RWKV_PAIRS = 4
LORA_PAD = 128
MIB = 1024 * 1024


def _params(semantics, vmem_mib):
    return pltpu.CompilerParams(dimension_semantics=semantics, vmem_limit_bytes=vmem_mib * MIB)


def _dot(a, b, precision=None):
    return jnp.dot(a, b, preferred_element_type=F32, precision=precision)


def _dot_bf16(a, b):
    return jnp.dot(a.astype(BF16), b.astype(BF16), preferred_element_type=F32)


def _dot_nt(a, b, precision=None):
    return lax.dot_general(a, b, (((1,), (1,)), ((), ())), preferred_element_type=F32, precision=precision)


def _split3(x):
    hi = x.astype(BF16)
    rem = x - hi.astype(F32)
    mid = rem.astype(BF16)
    lo = (rem - mid.astype(F32)).astype(BF16)
    return hi, mid, lo


def _dot_exact_lhs(m, x):
    hi, mid, lo = _split3(x)
    return _dot(m, lo) + _dot(m, mid) + _dot(m, hi)


def _dot_exact_rhs(x, m):
    hi, mid, lo = _split3(x)
    return _dot(lo, m) + _dot(mid, m) + _dot(hi, m)


def _dot_x3(a, b):
    a_hi, b_hi = a.astype(BF16), b.astype(BF16)
    a_lo, b_lo = (a - a_hi.astype(F32)).astype(BF16), (b - b_hi.astype(F32)).astype(BF16)
    return _dot(a_lo, b_hi) + _dot(a_hi, b_lo) + _dot(a_hi, b_hi)


def _sigmoid(x):
    return 1.0 / (1.0 + jnp.exp(-x))


def _softplus(x):
    return jnp.maximum(x, 0.0) + jnp.log(1.0 + jnp.exp(-jnp.abs(x)))


def _tile(n, target):
    t = min(n, target) // PAIR * PAIR
    while n % t:
        t -= PAIR
    return t


def _head_sum_matrix():
    r = lax.broadcasted_iota(jnp.int32, (PAIR, PAIR), 0) // HEAD_DIM
    c = lax.broadcasted_iota(jnp.int32, (PAIR, PAIR), 1) // HEAD_DIM
    return (r == c).astype(F32)


def _rmsnorm_kernel(x_ref, g_ref, o_ref):
    x = x_ref[...]
    ms = jnp.mean(x * x, axis=-1, keepdims=True)
    o_ref[...] = (x * lax.rsqrt(ms + RMS_EPS) * g_ref[...]).astype(o_ref.dtype)


def rmsnorm(x, g, out_dtype, tm=256):
    m, d = x.shape
    return pl.pallas_call(
        _rmsnorm_kernel,
        grid=(m // tm,),
        in_specs=[pl.BlockSpec((tm, d), lambda i: (i, 0)), pl.BlockSpec((1, d), lambda i: (0, 0))],
        out_specs=pl.BlockSpec((tm, d), lambda i: (i, 0)),
        out_shape=jax.ShapeDtypeStruct((m, d), out_dtype),
        compiler_params=_params(("parallel",), 40),
        name="rmsnorm",
    )(x, g.reshape(1, d))


def _matmul_sum_kernel(*refs, n_ops, has_res):
    a_refs, b_refs = refs[:n_ops], refs[n_ops:2 * n_ops]
    o_ref = refs[-1]
    acc = _dot(a_refs[0][...], b_refs[0][...])
    for a_ref, b_ref in zip(a_refs[1:], b_refs[1:]):
        acc = acc + _dot(a_ref[...], b_ref[...])
    if has_res:
        acc = acc + refs[2 * n_ops][...]
    o_ref[...] = acc.astype(o_ref.dtype)


def matmul_sum(a_ops, b_ops, n_out, out_dtype, tm, tn, residual=None, layer=None, a_buffers=2, vmem_mib=48):
    m = a_ops[0][0].shape[0]
    tm, tn = _tile(m, tm), _tile(n_out, tn)
    in_specs, args = [], []
    a_mode = {} if a_buffers == 2 else {"pipeline_mode": pl.Buffered(a_buffers)}
    for arr, kw, kb in a_ops:
        in_specs.append(pl.BlockSpec((tm, kw), lambda i, j, kb=kb: (i, kb), **a_mode))
        args.append(arr)
    for arr, kw, kb in b_ops:
        if layer is None:
            in_specs.append(pl.BlockSpec((kw, tn), lambda i, j, kb=kb: (kb, j)))
        else:
            in_specs.append(pl.BlockSpec((None, kw, tn), lambda i, j, kb=kb: (layer, kb, j)))
        args.append(arr)
    if residual is not None:
        in_specs.append(pl.BlockSpec((tm, tn), lambda i, j: (i, j)))
        args.append(residual)
    return pl.pallas_call(
        functools.partial(_matmul_sum_kernel, n_ops=len(a_ops), has_res=residual is not None),
        grid=(m // tm, n_out // tn),
        in_specs=in_specs,
        out_specs=pl.BlockSpec((tm, tn), lambda i, j: (i, j)),
        out_shape=jax.ShapeDtypeStruct((m, n_out), out_dtype),
        compiler_params=_params(("parallel", "parallel"), vmem_mib),
        name="matmul_sum",
    )(*args)


def _swiglu_kernel(x_ref, wg_ref, wu_ref, o_ref):
    x = x_ref[...]
    gate = _dot(x, wg_ref[...])
    up = _dot(x, wu_ref[...])
    o_ref[...] = (gate * _sigmoid(gate) * up).astype(o_ref.dtype)


def swiglu_hidden(x, wg, wu, layer, tm, tn):
    m, k = x.shape
    f = wg.shape[2]
    tm, tn = _tile(m, tm), _tile(f, tn)
    w_spec = pl.BlockSpec((None, k, tn), lambda i, j: (layer, 0, j))
    return pl.pallas_call(
        _swiglu_kernel,
        grid=(m // tm, f // tn),
        in_specs=[pl.BlockSpec((tm, k), lambda i, j: (i, 0)), w_spec, w_spec],
        out_specs=pl.BlockSpec((tm, tn), lambda i, j: (i, j)),
        out_shape=jax.ShapeDtypeStruct((m, f), BF16),
        compiler_params=_params(("parallel", "parallel"), 48),
        name="swiglu_hidden",
    )(x, wg, wu)


def _band_mask(max_dist, has_prev):
    row = lax.broadcasted_iota(jnp.int32, (BLOCK, 2 * BLOCK), 0)
    col = lax.broadcasted_iota(jnp.int32, (BLOCK, 2 * BLOCK), 1)
    dist = row + BLOCK - col
    return (dist >= 0) & (dist <= max_dist) & ((col >= BLOCK) | has_prev)


def _pair_attention_group(qs, ks, vs, mask, head0, sinks=None):
    g = len(qs)
    masked_q = [jnp.where(head0 if e == 0 else jnp.logical_not(head0), q, jnp.zeros_like(q))
                for q in qs for e in range(2)]
    if len(ks) == 1:
        s = _dot_nt(jnp.concatenate(masked_q, axis=0), ks[0]).reshape(2 * g, BLOCK, 2 * BLOCK)
    else:
        s = jnp.stack([_dot_nt(q, ks[u // 2]) for u, q in enumerate(masked_q)])
    s = jnp.where(mask[None], s, NEG_INF)
    mx = jnp.max(s, axis=-1, keepdims=True)
    if sinks is not None:
        sk = jnp.stack(sinks)
        mx = jnp.maximum(mx, sk)
    pe = jnp.exp(s - mx)
    den = jnp.sum(pe, axis=-1, keepdims=True)
    if sinks is not None:
        den = den + jnp.exp(sk - mx)
    probs = pe.astype(BF16)
    inv_den = 1.0 / den
    lse = mx + jnp.log(den)

    def stack_heads(v):
        return jnp.concatenate([jnp.where(head0, v, jnp.zeros_like(v)), jnp.where(head0, jnp.zeros_like(v), v)], axis=0)

    p_cat = [jnp.concatenate([probs[2 * u], probs[2 * u + 1]], axis=1) for u in range(g)]
    if len(vs) == 1:
        o_all = _dot(jnp.concatenate(p_cat, axis=0), stack_heads(vs[0]))
        outs = [o_all[u * BLOCK:(u + 1) * BLOCK] for u in range(g)]
    else:
        outs = [_dot(p, stack_heads(v)) for p, v in zip(p_cat, vs)]
    return [(o * jnp.where(head0, inv_den[2 * u], inv_den[2 * u + 1]), jnp.where(head0, lse[2 * u], lse[2 * u + 1]))
            for u, o in enumerate(outs)]


def _swa_kernel(q_ref, kc_ref, kp_ref, vc_ref, vp_ref, sink_ref, o_ref, *, n_pairs, rep, max_dist):
    mask = _band_mask(max_dist, pl.program_id(1) > 0)
    head0 = lax.broadcasted_iota(jnp.int32, (1, PAIR), 1) < HEAD_DIM
    for kv in range(n_pairs // rep):
        kl = pl.ds(kv * PAIR, PAIR)
        k = jnp.concatenate([kp_ref[:, kl], kc_ref[:, kl]], axis=0)
        v = jnp.concatenate([vp_ref[:, kl], vc_ref[:, kl]], axis=0)
        pairs = range(kv * rep, (kv + 1) * rep)
        sinks = [sink_ref[0:1, pl.ds(p * PAIR + e * HEAD_DIM, 1)] for p in pairs for e in range(2)]
        outs = _pair_attention_group([q_ref[:, pl.ds(p * PAIR, PAIR)] for p in pairs], [k], [v], mask, head0, sinks)
        for p, (o, _) in zip(pairs, outs):
            o_ref[:, pl.ds(p * PAIR, PAIR)] = o.astype(o_ref.dtype)


def swa_attention(qkv, q_col, k_col, v_col, batch, seq, n_pairs, n_kv_pairs, max_dist, sink):
    nb = seq // BLOCK
    qw, kw = n_pairs * PAIR, n_kv_pairs * PAIR
    cur = lambda c: (lambda b, i: (b * nb + i, c))
    prev = lambda c: (lambda b, i: (b * nb + jnp.maximum(i - 1, 0), c))
    return pl.pallas_call(
        functools.partial(_swa_kernel, n_pairs=n_pairs, rep=n_pairs // n_kv_pairs, max_dist=max_dist),
        grid=(batch, nb),
        in_specs=[pl.BlockSpec((BLOCK, qw), cur(q_col)),
                  pl.BlockSpec((BLOCK, kw), cur(k_col)), pl.BlockSpec((BLOCK, kw), prev(k_col)),
                  pl.BlockSpec((BLOCK, kw), cur(v_col)), pl.BlockSpec((BLOCK, kw), prev(v_col)),
                  pl.BlockSpec((1, qw), lambda b, i: (0, 0))],
        out_specs=pl.BlockSpec((BLOCK, qw), cur(0)),
        out_shape=jax.ShapeDtypeStruct((batch * seq, qw), BF16),
        compiler_params=_params(("parallel", "parallel"), 32),
        name="swa_attention",
    )(qkv, qkv, qkv, qkv, qkv, sink)


def _dilated_kernel(q_ref, kc_ref, kp_ref, vc_ref, vp_ref, y_ref, o_ref, lse_ref):
    slab = q_ref.shape[0]
    not_first = pl.program_id(1) > 0
    head0 = lax.broadcasted_iota(jnp.int32, (1, PAIR), 1) < HEAD_DIM
    for branch, (window, dil) in enumerate(DILATED_PAIRS):
        span = BLOCK * dil
        inner_mask = _band_mask(window // dil, True)
        edge_mask = _band_mask(window // dil, not_first)
        units = [(blk, p) for blk in range(slab // span) for p in range(dil)]
        edge_units = [u for u in units if u[0] == 0]
        inner_units = [u for u in units if u[0] > 0]
        groups = [(edge_units[g:g + DILATED_GROUP], edge_mask) for g in range(0, len(edge_units), DILATED_GROUP)]
        groups += [(inner_units[g:g + DILATED_GROUP], inner_mask) for g in range(0, len(inner_units), DILATED_GROUP)]
        for group, mask in groups:
            qs, ks, vs, rows = [], [], [], []
            for blk, p in group:
                cur = pl.ds(blk * span + p, BLOCK, stride=dil)
                if blk:
                    before = pl.ds((blk - 1) * span + p, BLOCK, stride=dil)
                    k_prev, v_prev = kc_ref[before, :], vc_ref[before, :]
                else:
                    before = pl.ds(slab - span + p, BLOCK, stride=dil)
                    k_prev, v_prev = kp_ref[before, :], vp_ref[before, :]
                qs.append(q_ref[cur, :].astype(BF16))
                ks.append(jnp.concatenate([k_prev, kc_ref[cur, :]], axis=0).astype(BF16))
                vs.append(jnp.concatenate([v_prev, vc_ref[cur, :]], axis=0).astype(BF16))
                rows.append(cur)
            for cur, (o, lse) in zip(rows, _pair_attention_group(qs, ks, vs, mask, head0)):
                o_ref[branch, cur, :] = o
                lse_ref[branch, cur, :] = lse
    l1, l2, l3 = lse_ref[0], lse_ref[1], lse_ref[2]
    mx = jnp.maximum(jnp.maximum(l1, l2), l3)
    e1, e2, e3 = jnp.exp(l1 - mx), jnp.exp(l2 - mx), jnp.exp(l3 - mx)
    num = e1 * o_ref[0] + e2 * o_ref[1] + e3 * o_ref[2]
    y_ref[...] = (num / (e1 + e2 + e3)).astype(y_ref.dtype)


def dilated_attention(qkv, batch, seq, n_pairs):
    slab = BLOCK * max(dil for _, dil in DILATED_PAIRS)
    ns = seq // slab
    cur = lambda base: (lambda b, s, p: (b * ns + s, base + p))
    prev = lambda base: (lambda b, s, p: (b * ns + jnp.maximum(s - 1, 0), base + p))
    block = lambda index_map: pl.BlockSpec((slab, PAIR), index_map)
    n_branches = len(DILATED_PAIRS)
    return pl.pallas_call(
        _dilated_kernel,
        grid=(batch, ns, n_pairs),
        in_specs=[block(cur(0)), block(cur(n_pairs)), block(prev(n_pairs)),
                  block(cur(2 * n_pairs)), block(prev(2 * n_pairs))],
        out_specs=block(cur(0)),
        out_shape=jax.ShapeDtypeStruct((batch * seq, n_pairs * PAIR), BF16),
        scratch_shapes=[pltpu.VMEM((n_branches, slab, PAIR), F32), pltpu.VMEM((n_branches, slab, PAIR), F32)],
        compiler_params=_params(("parallel", "parallel", "parallel"), 40),
        name="dilated_attention",
    )(qkv, qkv, qkv, qkv, qkv)


def _forget_cumsum_kernel(f_ref, b_ref, c_ref, kb_ref, carry_ref):
    @pl.when(pl.program_id(1) == 0)
    def _():
        carry_ref[...] = jnp.zeros_like(carry_ref)

    blk = f_ref.shape[0]
    log_f = -_softplus(-(f_ref[...] + b_ref[...]))
    tri = (lax.broadcasted_iota(jnp.int32, (blk, blk), 0) >= lax.broadcasted_iota(jnp.int32, (blk, blk), 1))
    c = _dot_exact_lhs(tri.astype(BF16), log_f) + carry_ref[...]
    c_ref[...] = c
    carry_ref[...] = c[blk - 1:blk, :]

    width = kb_ref.shape[1]
    head = lax.broadcasted_iota(jnp.int32, (c.shape[1], width), 0)
    out = lax.broadcasted_iota(jnp.int32, (c.shape[1], width), 1)
    pair, lane = out // PAIR, out % PAIR
    placed = None
    for piece, part in enumerate(_split3(-c)):
        select = ((head == 2 * pair) & (lane == piece)) | ((head == 2 * pair + 1) & (lane == 3 + piece))
        term = _dot(part, select.astype(BF16))
        placed = term if placed is None else placed + term
    kb_ref[...] = placed.astype(BF16)


def forget_cumsum(fc, b_f, batch, seq, n_pairs, blk=512):
    w = fc.shape[1]
    nb = seq // blk
    row_block = lambda width: pl.BlockSpec((blk, width), lambda b, i: (b * nb + i, 0))
    return pl.pallas_call(
        _forget_cumsum_kernel,
        grid=(batch, nb),
        in_specs=[row_block(w), pl.BlockSpec((1, w), lambda b, i: (0, 0))],
        out_specs=[row_block(w), row_block(n_pairs * PAIR)],
        out_shape=[jax.ShapeDtypeStruct(fc.shape, F32), jax.ShapeDtypeStruct((fc.shape[0], n_pairs * PAIR), BF16)],
        scratch_shapes=[pltpu.VMEM((1, w), F32)],
        compiler_params=_params(("parallel", "arbitrary"), 16),
        name="forget_cumsum",
    )(fc, b_f)


FOX_ROWS = 32
FOX_PAIRS = 4


def _fox_kernel(qi_ref, kj_ref, q_ref, k_ref, v_ref, cb_ref, cq_ref, o_ref,
                m_ref, al_ref, cqr_ref, acc_ref, s_ref, p_ref):
    t = pl.program_id(2)
    i, j = qi_ref[t], kj_ref[t]
    tq, tk = q_ref.shape[0], k_ref.shape[0]
    n_tiles = tk // PAIR
    heads = range(2 * FOX_PAIRS)
    lane1 = lax.broadcasted_iota(jnp.int32, (1, PAIR), 1)
    head0 = lane1 < HEAD_DIM

    @pl.when(j == 0)
    def _():
        m_ref[...] = jnp.full_like(m_ref, NEG_INF)
        acc_ref[...] = jnp.zeros_like(acc_ref)
        c_blk = cq_ref[...]
        head_row = lax.broadcasted_iota(jnp.int32, (PAIR, PAIR), 0)
        for h in heads:
            pick = (head_row == pl.program_id(1) * len(heads) + h).astype(BF16)
            cqr_ref[h] = _dot_exact_rhs(c_blk, pick)

    def step(masked):
        lane = lax.broadcasted_iota(jnp.int32, (FOX_ROWS, PAIR), 1)
        sub = lax.broadcasted_iota(jnp.int32, (FOX_ROWS, PAIR), 0)
        for h in heads:
            g, e = divmod(h, 2)
            pair = pl.ds(g * PAIR, PAIR)
            q = q_ref[:, pair] * (HEAD_DIM ** -0.5)
            sel = head0 if e == 0 else jnp.logical_not(head0)
            bias_on = jnp.broadcast_to(((lane1 >= 3 * e) & (lane1 < 3 * e + 3)).astype(BF16), (tq, PAIR))
            s_ref[h] = _dot_nt(jnp.concatenate([jnp.where(sel, q, jnp.zeros_like(q)), bias_on], axis=1),
                               jnp.concatenate([k_ref[:, pair], cb_ref[:, pair]], axis=1))
        pv = []
        for h in heads:
            g, e = divmod(h, 2)
            sel = head0 if e == 0 else jnp.logical_not(head0)
            for c in range(tq // FOX_ROWS):
                rows = pl.ds(c * FOX_ROWS, FOX_ROWS)
                tiles = []
                for n in range(n_tiles):
                    s = s_ref[h, rows, pl.ds(n * PAIR, PAIR)]
                    if masked:
                        s = jnp.where(lane + n * PAIR <= sub + c * FOX_ROWS, s, NEG_INF)
                    tiles.append(s)
                mx = tiles[0]
                for s in tiles[1:]:
                    mx = jnp.maximum(mx, s)
                cq = cqr_ref[h, rows, :]
                m_prev = m_ref[h, rows, :]
                m_new = jnp.maximum(m_prev, jnp.max(mx, axis=-1, keepdims=True) + cq)
                shift = cq - m_new
                for n, s in enumerate(tiles):
                    p_ref[rows, pl.ds(h * tk + n * PAIR, PAIR)] = jnp.exp(s + shift).astype(BF16)
                m_ref[h, rows, :] = m_new
                al_ref[h, rows, :] = jnp.exp(m_prev - m_new)
            v = v_ref[:, pl.ds(g * PAIR, PAIR)]
            v_aug = jnp.concatenate([jnp.where(sel, v, jnp.zeros_like(v)),
                                     jnp.broadcast_to((lane1 == e).astype(BF16), (tk, PAIR))], axis=1)
            pv.append(_dot(p_ref[:, pl.ds(h * tk, tk)], v_aug))
        for g in range(FOX_PAIRS):
            a0, a1 = al_ref[2 * g], al_ref[2 * g + 1]
            alpha = jnp.concatenate([jnp.where(head0, a0, a1), jnp.where(lane1 == 0, a0, a1)], axis=1)
            acc = acc_ref[g] * alpha + pv[2 * g] + pv[2 * g + 1]
            if masked:
                den = jnp.where(head0, acc[:, PAIR:PAIR + 1], acc[:, PAIR + 1:PAIR + 2])
                o_ref[:, pl.ds(g * PAIR, PAIR)] = (acc[:, :PAIR] / den).astype(o_ref.dtype)
            else:
                acc_ref[g] = acc

    @pl.when(j < i)
    def _():
        step(False)

    @pl.when(j == i)
    def _():
        step(True)


def fox_attention(qkv, key_bias, c, batch, seq, n_pairs, q_col, k_col, v_col, tq):
    nq = seq // tq
    qi = jnp.asarray([i for i in range(nq) for _ in range(i + 1)], jnp.int32)
    kj = jnp.asarray([j for i in range(nq) for j in range(i + 1)], jnp.int32)
    gw, heads = FOX_PAIRS * PAIR, 2 * FOX_PAIRS
    assert all(x % FOX_PAIRS == 0 for x in (n_pairs, q_col, k_col, v_col)) and seq % tq == 0
    rep = pltpu.VMEM((heads, tq, PAIR), F32)
    grid_spec = pltpu.PrefetchScalarGridSpec(
        num_scalar_prefetch=2,
        grid=(batch, n_pairs // FOX_PAIRS, qi.shape[0]),
        in_specs=[pl.BlockSpec((tq, gw), lambda b, p, t, qi, kj: (b * nq + qi[t], q_col // FOX_PAIRS + p)),
                  pl.BlockSpec((tq, gw), lambda b, p, t, qi, kj: (b * nq + kj[t], k_col // FOX_PAIRS + p)),
                  pl.BlockSpec((tq, gw), lambda b, p, t, qi, kj: (b * nq + kj[t], v_col // FOX_PAIRS + p)),
                  pl.BlockSpec((tq, gw), lambda b, p, t, qi, kj: (b * nq + kj[t], p)),
                  pl.BlockSpec((tq, PAIR), lambda b, p, t, qi, kj: (b * nq + qi[t], 0))],
        out_specs=pl.BlockSpec((tq, gw), lambda b, p, t, qi, kj: (b * nq + qi[t], p)),
        scratch_shapes=[rep, rep, rep, pltpu.VMEM((FOX_PAIRS, tq, 2 * PAIR), F32),
                        pltpu.VMEM((heads, tq, tq), F32), pltpu.VMEM((tq, heads * tq), BF16)])
    return pl.pallas_call(
        _fox_kernel,
        grid_spec=grid_spec,
        out_shape=jax.ShapeDtypeStruct((batch * seq, n_pairs * PAIR), BF16),
        compiler_params=_params(("parallel", "parallel", "arbitrary"), 32),
        name="fox_attention",
    )(qi, kj, qkv, qkv, qkv, key_bias, c)


def _rwkv_prep_kernel(x_ref, xp_ref, xl_ref, xlp_ref, mix_ref, mixl_ref, w0_ref, w2_ref, a0_ref, a2_ref, g2_ref,
                      kk_ref, ka_ref, r_ref, lw_ref, k_ref, v_ref, a_ref, b_ref, g_ref, *, dim):
    first = pl.program_id(1) == 0
    row = lax.broadcasted_iota(jnp.int32, (x_ref.shape[0], 1), 0)

    def shift_lerp(cur_ref, prev_ref, mix):
        x = cur_ref[...]
        prev_row = jnp.where(first, 0.0, prev_ref[7:8, :])
        shifted = jnp.where(row == 0, prev_row, pltpu.roll(x, 1, 0))
        return x + (shifted - x) * mix

    x = shift_lerp(x_ref, xp_ref, mix_ref[...])
    xl = shift_lerp(xl_ref, xlp_ref, mixl_ref[...])
    r, k, v = x[:, 0:dim], x[:, dim:2 * dim], x[:, 2 * dim:3 * dim]
    xw, xa, xg = xl[:, 0:LORA_PAD], xl[:, LORA_PAD:2 * LORA_PAD], xl[:, 2 * LORA_PAD:]
    w_log = -_softplus(-(w0_ref[...] + _dot_x3(jnp.tanh(xw), w2_ref[...]))) - 0.5
    a = _sigmoid(a0_ref[...] + _dot_x3(xa, a2_ref[...]))
    g = _dot_x3(_sigmoid(xg), g2_ref[...])

    kk = k * kk_ref[...]
    ones = _head_sum_matrix().astype(BF16)
    norm = []
    for p in range(dim // PAIR):
        kp = kk[:, p * PAIR:(p + 1) * PAIR]
        norm.append(jnp.sqrt(_dot_exact_rhs(kp * kp, ones)))
    kk = kk / jnp.maximum(jnp.concatenate(norm, axis=1), 1e-12)

    r_ref[...] = r.astype(r_ref.dtype)
    lw_ref[...] = -jnp.exp(w_log)
    k_ref[...] = (k * (1.0 + (a - 1.0) * ka_ref[...])).astype(k_ref.dtype)
    v_ref[...] = v.astype(v_ref.dtype)
    a_ref[...] = (-kk).astype(a_ref.dtype)
    b_ref[...] = (kk * a).astype(b_ref.dtype)
    g_ref[...] = g.astype(g_ref.dtype)


def rwkv_prep(pa, pl_, mix, mix_l, w0, w2, a0, a2, g2, k_k, k_a, batch, seq, dim, tt=128):
    m = pa.shape[0]
    nt = seq // tt
    full = lambda arr: pl.BlockSpec(arr.shape, lambda b, i: (0, 0))
    cur = lambda arr: pl.BlockSpec((tt, arr.shape[1]), lambda b, i: (b * nt + i, 0))
    before = lambda arr: pl.BlockSpec((8, arr.shape[1]),
                                      lambda b, i: (jnp.maximum((b * nt + i) * (tt // 8) - 1, 0), 0))
    out_spec = pl.BlockSpec((tt, dim), lambda b, i: (b * nt + i, 0))
    consts = [mix, mix_l, w0, w2, a0, a2, g2, k_k, k_a]
    return pl.pallas_call(
        functools.partial(_rwkv_prep_kernel, dim=dim),
        grid=(batch, nt),
        in_specs=[cur(pa), before(pa), cur(pl_), before(pl_)] + [full(c) for c in consts],
        out_specs=[out_spec] * 7,
        out_shape=[jax.ShapeDtypeStruct((m, dim), F32 if name == "lw" else BF16)
                   for name in ("r", "lw", "k", "v", "a", "b", "g")],
        compiler_params=_params(("parallel", "parallel"), 48),
        name="rwkv_prep",
    )(pa, pa, pl_, pl_, *consts)


def _stack_heads(x, head0):
    zero = jnp.zeros_like(x)
    return jnp.concatenate([jnp.where(head0, x, zero), jnp.where(head0, zero, x)], axis=0)


def _rwkv_scan_kernel(r_ref, lw_ref, k_ref, v_ref, a_ref, b_ref, g_ref, rk_ref, lnw_ref, lnb_ref,
                      o_ref, s_ref, y_ref):
    c_len = RWKV_CHUNK
    tt = r_ref.shape[0]
    chunks = range(tt // c_len)

    @pl.when(pl.program_id(2) == 0)
    def _():
        s_ref[...] = jnp.zeros_like(s_ref)

    head0 = lax.broadcasted_iota(jnp.int32, (1, PAIR), 1) < HEAD_DIM
    n2 = 2 * c_len
    row = lax.broadcasted_iota(jnp.int32, (n2, n2), 0)
    col = lax.broadcasted_iota(jnp.int32, (n2, n2), 1)
    t_loc, j_loc = row % c_len, col % c_len
    same_head = (row // c_len) == (col // c_len)
    strict = same_head & (j_loc < t_loc)
    incl = same_head & (j_loc <= t_loc)
    incl2 = jnp.concatenate([incl, incl], axis=1)
    sub_block = same_head & ((t_loc // 16) == (j_loc // 16))
    eye = (row == col).astype(F32)
    zeros = jnp.zeros((n2, PAIR), F32)
    tri = (lax.broadcasted_iota(jnp.int32, (c_len, c_len), 0)
           >= lax.broadcasted_iota(jnp.int32, (c_len, c_len), 1)).astype(BF16)

    def rows(x, c):
        return x[c * c_len:(c + 1) * c_len]

    pairs = range(r_ref.shape[1] // PAIR)
    units = [(g, c) for c in chunks for g in pairs]

    def load(ref, g):
        return ref[:, pl.ds(g * PAIR, PAIR)]

    lw = [load(lw_ref, g) for g in pairs]
    r, k, v, a, b = ([load(ref, g).astype(F32) for g in pairs] for ref in (r_ref, k_ref, v_ref, a_ref, b_ref))
    cw = [jnp.concatenate([_dot_exact_lhs(tri, rows(lw[g], c)) for c in chunks], axis=0) for g in pairs]
    e_pos, e_neg = [jnp.exp(x) for x in cw], [jnp.exp(-x) for x in cw]
    a_t = [a[g] * jnp.exp(cw[g] - lw[g]) for g in pairs]
    r_t = [r[g] * e_pos[g] for g in pairs]
    b_t = [b[g] * e_neg[g] for g in pairs]
    k_t = [k[g] * e_neg[g] for g in pairs]

    e_last = [jnp.exp(rows(cw[g], c)[c_len - 1:c_len, :]) for g, c in units]
    a_s = [_stack_heads(rows(a_t[g], c), head0) for g, c in units]
    r_s = [_stack_heads(rows(r_t[g], c), head0) for g, c in units]
    v_s = [_stack_heads(rows(v[g], c), head0) for g, c in units]
    scores = [_dot_nt(jnp.concatenate([a_s[u], r_s[u]], axis=0).astype(BF16),
                      jnp.concatenate([_stack_heads(rows(b_t[g], c), head0),
                                       _stack_heads(rows(k_t[g], c), head0)], axis=0).astype(BF16))
              for u, (g, c) in enumerate(units)]
    l_ab = [jnp.where(strict, s[:n2, :n2], 0.0) for s in scores]

    d1 = [jnp.where(sub_block, x, 0.0) for x in l_ab]
    d2 = [_dot_bf16(x, x) for x in d1]
    d4 = [_dot_bf16(x, x) for x in d2]
    d8 = [_dot_bf16(x, x) for x in d4]
    t_lo = [_dot_bf16(eye + x, eye + y) for x, y in zip(d1, d2)]
    t_hi = [_dot_bf16(eye + x, eye + y) for x, y in zip(d4, d8)]
    t_diag = [_dot_bf16(x, y) for x, y in zip(t_lo, t_hi)]
    n1 = [_dot_bf16(t, x - d) for t, x, d in zip(t_diag, l_ab, d1)]
    n_sq = [_dot_bf16(x, x) for x in n1]
    t_rest = [_dot_bf16(eye + x, eye + y) for x, y in zip(n1, n_sq)]
    t_inv = [_dot_bf16(x, t) for x, t in zip(t_rest, t_diag)]

    lv = [_dot_bf16(jnp.where(strict, s[:n2, n2:], 0.0), vs) for s, vs in zip(scores, v_s)]
    au = [_dot_bf16(t, jnp.concatenate([x, y], axis=1)) for t, x, y in zip(t_inv, a_s, lv)]
    z = []
    for u, (g, c) in enumerate(units):
        e_end = e_last[u] * rows(e_neg[g], c)
        bend_s = _stack_heads(rows(b[g], c) * e_end, head0)
        kend_s = _stack_heads(rows(k[g], c) * e_end, head0)
        lhs = jnp.concatenate([jnp.where(incl2, scores[u][n2:], 0.0),
                               jnp.concatenate([bend_s.T, kend_s.T], axis=1)], axis=0)
        rhs = jnp.concatenate([au[u], jnp.concatenate([zeros, v_s[u]], axis=1)], axis=0)
        z.append(_dot_bf16(lhs, rhs))

    state = [s_ref[g] for g in pairs]
    for u, (g, c) in enumerate(units):
        r_hat = r_s[u] + z[u][:n2, :PAIR]
        y_hat = z[u][:n2, PAIR:]
        lhs = jnp.concatenate([r_hat[:c_len] + r_hat[c_len:], z[u][n2:, :PAIR] + eye * e_last[u]], axis=0)
        prod = _dot_bf16(lhs, state[g])
        y_ref[pl.ds(c * c_len, c_len), pl.ds(g * PAIR, PAIR)] = prod[:c_len] + y_hat[:c_len] + y_hat[c_len:]
        state[g] = prod[c_len:] + z[u][n2:, PAIR:]
    for g in pairs:
        s_ref[g] = state[g]

    ones = _head_sum_matrix().astype(BF16)
    for g in pairs:
        y = load(y_ref, g)
        mu = _dot_exact_rhs(y, ones) * (1.0 / HEAD_DIM)
        d = y - mu
        var = _dot_exact_rhs(d * d, ones) * (1.0 / HEAD_DIM)
        yn = d * lax.rsqrt(var + RWKV_LN_EPS) * load(lnw_ref, g) + load(lnb_ref, g)
        bonus = _dot_exact_rhs(r[g] * k[g] * load(rk_ref, g), ones) * v[g]
        o_ref[:, pl.ds(g * PAIR, PAIR)] = ((yn + bonus) * load(g_ref, g)).astype(o_ref.dtype)


def rwkv_scan(r, lw, k, v, a, b, g, r_k, ln_w, ln_b, batch, seq, tt=512):
    m, dim = r.shape
    width = RWKV_PAIRS * PAIR
    assert dim % width == 0 and seq % tt == 0, (dim, seq)
    nt = seq // tt
    seq_spec = pl.BlockSpec((tt, width), lambda bb, p, i: (bb * nt + i, p))
    vec_spec = pl.BlockSpec((1, width), lambda bb, p, i: (0, p))
    return pl.pallas_call(
        _rwkv_scan_kernel,
        grid=(batch, dim // width, nt),
        in_specs=[seq_spec] * 7 + [vec_spec] * 3,
        out_specs=seq_spec,
        out_shape=jax.ShapeDtypeStruct((m, dim), BF16),
        scratch_shapes=[pltpu.VMEM((RWKV_PAIRS, PAIR, PAIR), F32), pltpu.VMEM((tt, width), F32)],
        compiler_params=_params(("parallel", "parallel", "arbitrary"), 32),
        name="rwkv_scan",
    )(r, lw, k, v, a, b, g, r_k, ln_w, ln_b)


def _pad_cols(w, width):
    return jnp.pad(w, ((0, 0), (0, width - w.shape[1])))


def _pad_rows(w, height):
    return jnp.pad(w, ((0, height - w.shape[0]), (0, 0)))


def _dup_kv_heads(w):
    d, c = w.shape
    return jnp.tile(w.reshape(d, c // HEAD_DIM, 1, HEAD_DIM), (1, 1, 2, 1)).reshape(d, 2 * c)


def even_mixer(h, xn, batch, seq, w_in, mix, w0, w2, a0, a2, g2, k_k, k_a, r_k, ln_w, ln_b, sink, w_out, layer,
               dim, swa_heads, swa_kv_heads, tm, tn):
    d_model = xn.shape[1]
    decay_lora, aaa_lora = w2.shape[0], a2.shape[0]
    o = 3 * dim
    o_a, o_g = o + decay_lora, o + decay_lora + aaa_lora
    rwkv_cols = o_g + g2.shape[0]
    scale = HEAD_DIM ** -0.5

    def lora_layout(t):
        return jnp.concatenate([_pad_cols(t[..., o:o_a], LORA_PAD), _pad_cols(t[..., o_a:o_g], LORA_PAD),
                                t[..., o_g:rwkv_cols]], axis=-1)

    w_layer = w_in[layer]
    w_l = lora_layout(w_layer)
    q_end = rwkv_cols + swa_heads * HEAD_DIM
    k_end = q_end + swa_kv_heads * HEAD_DIM
    w_b = jnp.concatenate([w_layer[:, rwkv_cols:q_end] * scale, _dup_kv_heads(w_layer[:, q_end:k_end]),
                           _dup_kv_heads(w_layer[:, k_end:])], axis=1)

    pa = matmul_sum([(xn, d_model, 0)], [(w_in, d_model, 0)], o, F32, tm, tn, layer=layer)
    pa_l = matmul_sum([(xn, d_model, 0)], [(w_l, d_model, 0)], w_l.shape[1], F32, tm, tn)
    pb = matmul_sum([(xn, d_model, 0)], [(w_b, d_model, 0)], w_b.shape[1], BF16, tm, tn)

    row = lambda t: t.reshape(1, -1)
    prep = rwkv_prep(pa, pa_l, row(mix)[:, :o], lora_layout(row(mix)), row(w0), _pad_rows(w2, LORA_PAD), row(a0),
                     _pad_rows(a2, LORA_PAD), g2, row(k_k), row(k_a), batch, seq, dim)
    y_a = rwkv_scan(*prep, row(r_k), row(ln_w), row(ln_b), batch, seq)

    q_pairs, kv_pairs = swa_heads // 2, swa_kv_heads
    kv_w = kv_pairs * PAIR
    kv_base = (q_pairs * PAIR) // kv_w
    sink_lanes = jnp.repeat(sink, HEAD_DIM).reshape(1, -1)
    y_b = swa_attention(pb, 0, kv_base, kv_base + 1, batch, seq, q_pairs, kv_pairs, SWA_WINDOW - 1, sink_lanes)

    return _out_proj(h, y_a, y_b, w_out, layer, tm, tn)


def odd_mixer(h, xn, batch, seq, w_in, b_f, w_out, layer, fox_heads, dil_heads, tm, tn, fox_tq):
    d_model = xn.shape[1]
    cw, dw = fox_heads * HEAD_DIM, dil_heads * HEAD_DIM
    scale = HEAD_DIM ** -0.5
    f0 = 3 * cw
    d0 = f0 + fox_heads
    w_layer = w_in[layer]
    w_d = jnp.concatenate([w_layer[:, d0:d0 + dw] * scale, w_layer[:, d0 + dw:]], axis=1)
    w_f = _pad_cols(w_layer[:, f0:d0], PAIR)
    proj = matmul_sum([(xn, d_model, 0)], [(w_in, d_model, 0)], f0, BF16, tm, tn, layer=layer)
    proj_d = matmul_sum([(xn, d_model, 0)], [(w_d, d_model, 0)], 3 * dw, F32, tm, tn)
    fc = matmul_sum([(xn, d_model, 0)], [(w_f, d_model, 0)], PAIR, F32, tm, PAIR)

    fox_pairs = fox_heads // 2
    c, key_bias = forget_cumsum(fc, _pad_cols(b_f.reshape(1, -1), PAIR), batch, seq, fox_pairs)
    y_c = fox_attention(proj, key_bias, c, batch, seq, fox_pairs, 0, fox_pairs, 2 * fox_pairs, fox_tq)

    y_d = dilated_attention(proj_d, batch, seq, dil_heads // 2)

    return _out_proj(h, y_c, y_d, w_out, layer, tm, tn)


def _out_proj(h, y_1, y_2, w_out, layer, tm, tn):
    w1, w2 = y_1.shape[1], y_2.shape[1]
    kw = math.gcd(w1, w2)
    n1, n2 = w1 // kw, w2 // kw
    a_ops = [(y_1, kw, c) for c in range(n1)] + [(y_2, kw, c) for c in range(n2)]
    b_ops = [(w_out, kw, c) for c in range(n1 + n2)]
    return matmul_sum(a_ops, b_ops, w_out.shape[2], F32, tm, tn, residual=h, layer=layer)


def ffn(h, xn, w_gate, w_up, w_down, layer, tm, tn_hidden, tm_down, tn_down):
    half = w_gate.shape[2] // 2
    hidden = swiglu_hidden(xn, w_gate, w_up, layer, tm, tn_hidden)
    return matmul_sum([(hidden, half, 0), (hidden, half, 1)], [(w_down, half, 0), (w_down, half, 1)],
                      w_down.shape[2], F32, tm_down, tn_down, residual=h, layer=layer, a_buffers=1, vmem_mib=56)


def kernel(x, norm_mix, norm_ffn, norm_final, ffn_w_gate, ffn_w_up, ffn_w_down, ev_w_in, rwkv_mix, rwkv_w0, rwkv_w2, rwkv_a0, rwkv_a2, rwkv_g2, rwkv_k_k, rwkv_k_a, rwkv_r_k, rwkv_ln_w, rwkv_ln_b, swa_sink, ev_w_out, od_w_in, fox_b_f, od_w_out):
    batch, seq, d_model = x.shape
    depth = norm_mix.shape[0]
    dim = rwkv_w0.shape[1]
    swa_heads = swa_sink.shape[1]
    fox_heads = fox_b_f.shape[1]
    dil_heads = d_model // HEAD_DIM - fox_heads
    tm, tn = 1024, 512
    w_gate, w_up, w_down = ffn_w_gate.astype(BF16), ffn_w_up.astype(BF16), ffn_w_down.astype(BF16)
    w_out_even, w_out_odd = ev_w_out.astype(BF16), od_w_out.astype(BF16)
    w_in_even, w_in_odd = ev_w_in.astype(BF16), od_w_in.astype(BF16)
    h = x.reshape(batch * seq, d_model)
    for layer in range(depth):
        i = layer // 2
        xn = rmsnorm(h, norm_mix[layer], BF16)
        if layer % 2 == 0:
            h = even_mixer(h, xn, batch, seq, w_in_even, rwkv_mix[i], rwkv_w0[i], rwkv_w2[i], rwkv_a0[i], rwkv_a2[i],
                           rwkv_g2[i], rwkv_k_k[i], rwkv_k_a[i], rwkv_r_k[i].reshape(-1), rwkv_ln_w[i], rwkv_ln_b[i],
                           swa_sink[i], w_out_even, i, dim, swa_heads, swa_heads // 8, tm, tn)
        else:
            h = odd_mixer(h, xn, batch, seq, w_in_odd, fox_b_f[i], w_out_odd, i, fox_heads, dil_heads, tm, tn, 512)
        xn = rmsnorm(h, norm_ffn[layer], BF16)
        h = ffn(h, xn, w_gate, w_up, w_down, layer, tm, 256, tm, 256)
    return rmsnorm(h, norm_final, F32).reshape(batch, seq, d_model)
```

```python
import functools
import math

import jax
import jax.numpy as jnp
from jax import lax
from jax.experimental import pallas as pl
from jax.experimental.pallas import tpu as pltpu

F32 = jnp.float32
BF16 = jnp.bfloat16
HIGHEST = lax.Precision.HIGHEST

HEAD_DIM = 64
PAIR = 2 * HEAD_DIM
BLOCK = 128
SWA_WINDOW = 128
DILATED_PAIRS = ((128, 1), (512, 4), (2048, 16))
RMS_EPS = 1e-6
RWKV_LN_EPS = 64e-5
NEG_INF = -1e30
DILATED_GROUP = 8
RWKV_CHUNK = 64
RWKV_PAIRS = 4
LORA_PAD = 128
MIB = 1024 * 1024


def _params(semantics, vmem_mib):
    return pltpu.CompilerParams(dimension_semantics=semantics, vmem_limit_bytes=vmem_mib * MIB)


def _dot(a, b, precision=None):
    return jnp.dot(a, b, preferred_element_type=F32, precision=precision)


def _dot_bf16(a, b):
    return jnp.dot(a.astype(BF16), b.astype(BF16), preferred_element_type=F32)


def _dot_nt(a, b, precision=None):
    return lax.dot_general(a, b, (((1,), (1,)), ((), ())), preferred_element_type=F32, precision=precision)


def _split3(x):
    hi = x.astype(BF16)
    rem = x - hi.astype(F32)
    mid = rem.astype(BF16)
    lo = (rem - mid.astype(F32)).astype(BF16)
    return hi, mid, lo


def _dot_exact_lhs(m, x):
    hi, mid, lo = _split3(x)
    return _dot(m, lo) + _dot(m, mid) + _dot(m, hi)


def _dot_exact_rhs(x, m):
    hi, mid, lo = _split3(x)
    return _dot(lo, m) + _dot(mid, m) + _dot(hi, m)


def _dot_x3(a, b):
    a_hi, b_hi = a.astype(BF16), b.astype(BF16)
    a_lo, b_lo = (a - a_hi.astype(F32)).astype(BF16), (b - b_hi.astype(F32)).astype(BF16)
    return _dot(a_lo, b_hi) + _dot(a_hi, b_lo) + _dot(a_hi, b_hi)


def _sigmoid(x):
    return 1.0 / (1.0 + jnp.exp(-x))


def _softplus(x):
    return jnp.maximum(x, 0.0) + jnp.log(1.0 + jnp.exp(-jnp.abs(x)))


def _tile(n, target):
    t = min(n, target) // PAIR * PAIR
    while n % t:
        t -= PAIR
    return t


def _head_sum_matrix():
    r = lax.broadcasted_iota(jnp.int32, (PAIR, PAIR), 0) // HEAD_DIM
    c = lax.broadcasted_iota(jnp.int32, (PAIR, PAIR), 1) // HEAD_DIM
    return (r == c).astype(F32)


def _rmsnorm_kernel(x_ref, g_ref, o_ref):
    x = x_ref[...]
    ms = jnp.mean(x * x, axis=-1, keepdims=True)
    o_ref[...] = (x * lax.rsqrt(ms + RMS_EPS) * g_ref[...]).astype(o_ref.dtype)


def rmsnorm(x, g, out_dtype, tm=256):
    m, d = x.shape
    return pl.pallas_call(
        _rmsnorm_kernel,
        grid=(m // tm,),
        in_specs=[pl.BlockSpec((tm, d), lambda i: (i, 0)), pl.BlockSpec((1, d), lambda i: (0, 0))],
        out_specs=pl.BlockSpec((tm, d), lambda i: (i, 0)),
        out_shape=jax.ShapeDtypeStruct((m, d), out_dtype),
        compiler_params=_params(("parallel",), 40),
        name="rmsnorm",
    )(x, g.reshape(1, d))


def _split_norm_kernel(x_ref, g_ref, xg_ref, ss_ref):
    x = x_ref[...]
    xg_ref[...] = (x * g_ref[...]).astype(xg_ref.dtype)
    ss_ref[...] = jnp.broadcast_to(jnp.sum(x * x, axis=-1, keepdims=True), ss_ref.shape)


def split_norm(x, g, tm=256):
    m, d = x.shape
    return pl.pallas_call(
        _split_norm_kernel,
        grid=(m // tm,),
        in_specs=[pl.BlockSpec((tm, d), lambda i: (i, 0)), pl.BlockSpec((1, d), lambda i: (0, 0))],
        out_specs=[pl.BlockSpec((tm, d), lambda i: (i, 0)), pl.BlockSpec((tm, PAIR), lambda i: (i, 0))],
        out_shape=[jax.ShapeDtypeStruct((m, d), BF16), jax.ShapeDtypeStruct((m, PAIR), F32)],
        compiler_params=_params(("parallel",), 40),
        name="split_norm",
    )(x, g.reshape(1, d))


def _row_scale(ss_ref, width):
    return lax.rsqrt(ss_ref[:, 0:1] * (1.0 / width) + RMS_EPS)


def _matmul_sum_kernel(*refs, n_ops, norm_width, has_res, emit_norm):
    a_refs, b_refs = refs[:n_ops], refs[n_ops:2 * n_ops]
    rest = list(refs[2 * n_ops:])
    ss_ref = rest.pop(0) if norm_width else None
    res_ref = rest.pop(0) if has_res else None
    gain_ref = rest.pop(0) if emit_norm else None
    o_ref = rest.pop(0)
    acc = _dot(a_refs[0][...], b_refs[0][...])
    for a_ref, b_ref in zip(a_refs[1:], b_refs[1:]):
        acc = acc + _dot(a_ref[...], b_ref[...])
    if norm_width:
        acc = acc * _row_scale(ss_ref, norm_width)
    if has_res:
        acc = acc + res_ref[...]
    o_ref[...] = acc.astype(o_ref.dtype)
    if emit_norm:
        xg_ref, ss_out_ref = rest
        xg_ref[...] = (acc * gain_ref[...]).astype(xg_ref.dtype)
        part = jnp.broadcast_to(jnp.sum(acc * acc, axis=-1, keepdims=True), ss_out_ref.shape)

        @pl.when(pl.program_id(1) == 0)
        def _():
            ss_out_ref[...] = part

        @pl.when(pl.program_id(1) > 0)
        def _():
            ss_out_ref[...] += part


def matmul_sum(a_ops, b_ops, n_out, out_dtype, tm, tn, row_ss=None, residual=None, next_gain=None, layer=None,
               a_buffers=2, vmem_mib=48):
    m = a_ops[0][0].shape[0]
    tm, tn = _tile(m, tm), _tile(n_out, tn)
    in_specs, args = [], []
    a_mode = {} if a_buffers == 2 else {"pipeline_mode": pl.Buffered(a_buffers)}
    for arr, kw, kb in a_ops:
        in_specs.append(pl.BlockSpec((tm, kw), lambda i, j, kb=kb: (i, kb), **a_mode))
        args.append(arr)
    for arr, kw, kb in b_ops:
        if layer is None:
            in_specs.append(pl.BlockSpec((kw, tn), lambda i, j, kb=kb: (kb, j)))
        else:
            in_specs.append(pl.BlockSpec((None, kw, tn), lambda i, j, kb=kb: (layer, kb, j)))
        args.append(arr)
    norm_width = 0
    if row_ss is not None:
        norm_width = sum(kw for _, kw, _ in a_ops)
        in_specs.append(pl.BlockSpec((tm, PAIR), lambda i, j: (i, 0)))
        args.append(row_ss)
    tile = pl.BlockSpec((tm, tn), lambda i, j: (i, j))
    if residual is not None:
        in_specs.append(tile)
        args.append(residual)
    out_specs, out_shape = tile, jax.ShapeDtypeStruct((m, n_out), out_dtype)
    if next_gain is not None:
        in_specs.append(pl.BlockSpec((1, tn), lambda i, j: (0, j)))
        args.append(next_gain)
        out_specs = [tile, tile, pl.BlockSpec((tm, PAIR), lambda i, j: (i, 0))]
        out_shape = [out_shape, jax.ShapeDtypeStruct((m, n_out), BF16), jax.ShapeDtypeStruct((m, PAIR), F32)]
    return pl.pallas_call(
        functools.partial(_matmul_sum_kernel, n_ops=len(a_ops), norm_width=norm_width,
                          has_res=residual is not None, emit_norm=next_gain is not None),
        grid=(m // tm, n_out // tn),
        in_specs=in_specs,
        out_specs=out_specs,
        out_shape=out_shape,
        compiler_params=_params(("parallel", "arbitrary" if next_gain is not None else "parallel"), vmem_mib),
        name="matmul_sum",
    )(*args)


def _swiglu_kernel(x_ref, ss_ref, wg_ref, wu_ref, o_ref):
    x = x_ref[...]
    scale = _row_scale(ss_ref, x.shape[1])
    gate = _dot(x, wg_ref[...]) * scale
    up = _dot(x, wu_ref[...]) * scale
    o_ref[...] = (gate * _sigmoid(gate) * up).astype(o_ref.dtype)


def swiglu_hidden(x, row_ss, wg, wu, layer, tm, tn):
    m, k = x.shape
    f = wg.shape[2]
    tm, tn = _tile(m, tm), _tile(f, tn)
    w_spec = pl.BlockSpec((None, k, tn), lambda i, j: (layer, 0, j))
    return pl.pallas_call(
        _swiglu_kernel,
        grid=(m // tm, f // tn),
        in_specs=[pl.BlockSpec((tm, k), lambda i, j: (i, 0)), pl.BlockSpec((tm, PAIR), lambda i, j: (i, 0)),
                  w_spec, w_spec],
        out_specs=pl.BlockSpec((tm, tn), lambda i, j: (i, j)),
        out_shape=jax.ShapeDtypeStruct((m, f), BF16),
        compiler_params=_params(("parallel", "parallel"), 48),
        name="swiglu_hidden",
    )(x, row_ss, wg, wu)


def _band_mask(max_dist, has_prev):
    row = lax.broadcasted_iota(jnp.int32, (BLOCK, 2 * BLOCK), 0)
    col = lax.broadcasted_iota(jnp.int32, (BLOCK, 2 * BLOCK), 1)
    dist = row + BLOCK - col
    return (dist >= 0) & (dist <= max_dist) & ((col >= BLOCK) | has_prev)


def _pair_attention_group(qs, ks, vs, mask, head0, sinks=None):
    g = len(qs)
    masked_q = [jnp.where(head0 if e == 0 else jnp.logical_not(head0), q, jnp.zeros_like(q))
                for q in qs for e in range(2)]
    if len(ks) == 1:
        s = _dot_nt(jnp.concatenate(masked_q, axis=0), ks[0]).reshape(2 * g, BLOCK, 2 * BLOCK)
    else:
        s = jnp.stack([_dot_nt(q, ks[u // 2]) for u, q in enumerate(masked_q)])
    s = jnp.where(mask[None], s, NEG_INF)
    mx = jnp.max(s, axis=-1, keepdims=True)
    if sinks is not None:
        sk = jnp.stack(sinks)
        mx = jnp.maximum(mx, sk)
    pe = jnp.exp(s - mx)
    den = jnp.sum(pe, axis=-1, keepdims=True)
    if sinks is not None:
        den = den + jnp.exp(sk - mx)
    probs = pe.astype(BF16)
    inv_den = 1.0 / den
    lse = mx + jnp.log(den)

    def stack_heads(v):
        return jnp.concatenate([jnp.where(head0, v, jnp.zeros_like(v)), jnp.where(head0, jnp.zeros_like(v), v)], axis=0)

    p_cat = [jnp.concatenate([probs[2 * u], probs[2 * u + 1]], axis=1) for u in range(g)]
    if len(vs) == 1:
        o_all = _dot(jnp.concatenate(p_cat, axis=0), stack_heads(vs[0]))
        outs = [o_all[u * BLOCK:(u + 1) * BLOCK] for u in range(g)]
    else:
        outs = [_dot(p, stack_heads(v)) for p, v in zip(p_cat, vs)]
    return [(o * jnp.where(head0, inv_den[2 * u], inv_den[2 * u + 1]), jnp.where(head0, lse[2 * u], lse[2 * u + 1]))
            for u, o in enumerate(outs)]


def _swa_kernel(q_ref, kc_ref, kp_ref, vc_ref, vp_ref, sink_ref, o_ref, *, n_pairs, rep, max_dist):
    mask = _band_mask(max_dist, pl.program_id(1) > 0)
    head0 = lax.broadcasted_iota(jnp.int32, (1, PAIR), 1) < HEAD_DIM
    for kv in range(n_pairs // rep):
        kl = pl.ds(kv * PAIR, PAIR)
        k = jnp.concatenate([kp_ref[:, kl], kc_ref[:, kl]], axis=0)
        v = jnp.concatenate([vp_ref[:, kl], vc_ref[:, kl]], axis=0)
        pairs = range(kv * rep, (kv + 1) * rep)
        sinks = [sink_ref[0:1, pl.ds(p * PAIR + e * HEAD_DIM, 1)] for p in pairs for e in range(2)]
        outs = _pair_attention_group([q_ref[:, pl.ds(p * PAIR, PAIR)] for p in pairs], [k], [v], mask, head0, sinks)
        for p, (o, _) in zip(pairs, outs):
            o_ref[:, pl.ds(p * PAIR, PAIR)] = o.astype(o_ref.dtype)


def swa_attention(qkv, q_col, k_col, v_col, batch, seq, n_pairs, n_kv_pairs, max_dist, sink):
    nb = seq // BLOCK
    qw, kw = n_pairs * PAIR, n_kv_pairs * PAIR
    cur = lambda c: (lambda b, i: (b * nb + i, c))
    prev = lambda c: (lambda b, i: (b * nb + jnp.maximum(i - 1, 0), c))
    return pl.pallas_call(
        functools.partial(_swa_kernel, n_pairs=n_pairs, rep=n_pairs // n_kv_pairs, max_dist=max_dist),
        grid=(batch, nb),
        in_specs=[pl.BlockSpec((BLOCK, qw), cur(q_col)),
                  pl.BlockSpec((BLOCK, kw), cur(k_col)), pl.BlockSpec((BLOCK, kw), prev(k_col)),
                  pl.BlockSpec((BLOCK, kw), cur(v_col)), pl.BlockSpec((BLOCK, kw), prev(v_col)),
                  pl.BlockSpec((1, qw), lambda b, i: (0, 0))],
        out_specs=pl.BlockSpec((BLOCK, qw), cur(0)),
        out_shape=jax.ShapeDtypeStruct((batch * seq, qw), BF16),
        compiler_params=_params(("parallel", "parallel"), 32),
        name="swa_attention",
    )(qkv, qkv, qkv, qkv, qkv, sink)


def _dilated_kernel(q_ref, kc_ref, kp_ref, vc_ref, vp_ref, y_ref, o_ref, lse_ref):
    slab = q_ref.shape[0]
    not_first = pl.program_id(1) > 0
    head0 = lax.broadcasted_iota(jnp.int32, (1, PAIR), 1) < HEAD_DIM
    for branch, (window, dil) in enumerate(DILATED_PAIRS):
        span = BLOCK * dil
        inner_mask = _band_mask(window // dil, True)
        edge_mask = _band_mask(window // dil, not_first)
        units = [(blk, p) for blk in range(slab // span) for p in range(dil)]
        edge_units = [u for u in units if u[0] == 0]
        inner_units = [u for u in units if u[0] > 0]
        groups = [(edge_units[g:g + DILATED_GROUP], edge_mask) for g in range(0, len(edge_units), DILATED_GROUP)]
        groups += [(inner_units[g:g + DILATED_GROUP], inner_mask) for g in range(0, len(inner_units), DILATED_GROUP)]
        for group, mask in groups:
            qs, ks, vs, rows = [], [], [], []
            for blk, p in group:
                cur = pl.ds(blk * span + p, BLOCK, stride=dil)
                if blk:
                    before = pl.ds((blk - 1) * span + p, BLOCK, stride=dil)
                    k_prev, v_prev = kc_ref[before, :], vc_ref[before, :]
                else:
                    before = pl.ds(slab - span + p, BLOCK, stride=dil)
                    k_prev, v_prev = kp_ref[before, :], vp_ref[before, :]
                qs.append(q_ref[cur, :].astype(BF16))
                ks.append(jnp.concatenate([k_prev, kc_ref[cur, :]], axis=0).astype(BF16))
                vs.append(jnp.concatenate([v_prev, vc_ref[cur, :]], axis=0).astype(BF16))
                rows.append(cur)
            for cur, (o, lse) in zip(rows, _pair_attention_group(qs, ks, vs, mask, head0)):
                o_ref[branch, cur, :] = o
                lse_ref[branch, cur, :] = lse
    l1, l2, l3 = lse_ref[0], lse_ref[1], lse_ref[2]
    mx = jnp.maximum(jnp.maximum(l1, l2), l3)
    e1, e2, e3 = jnp.exp(l1 - mx), jnp.exp(l2 - mx), jnp.exp(l3 - mx)
    num = e1 * o_ref[0] + e2 * o_ref[1] + e3 * o_ref[2]
    y_ref[...] = (num / (e1 + e2 + e3)).astype(y_ref.dtype)


def dilated_attention(qkv, batch, seq, n_pairs):
    slab = BLOCK * max(dil for _, dil in DILATED_PAIRS)
    ns = seq // slab
    cur = lambda base: (lambda b, s, p: (b * ns + s, base + p))
    prev = lambda base: (lambda b, s, p: (b * ns + jnp.maximum(s - 1, 0), base + p))
    block = lambda index_map: pl.BlockSpec((slab, PAIR), index_map)
    n_branches = len(DILATED_PAIRS)
    return pl.pallas_call(
        _dilated_kernel,
        grid=(batch, ns, n_pairs),
        in_specs=[block(cur(0)), block(cur(n_pairs)), block(prev(n_pairs)),
                  block(cur(2 * n_pairs)), block(prev(2 * n_pairs))],
        out_specs=block(cur(0)),
        out_shape=jax.ShapeDtypeStruct((batch * seq, n_pairs * PAIR), BF16),
        scratch_shapes=[pltpu.VMEM((n_branches, slab, PAIR), F32), pltpu.VMEM((n_branches, slab, PAIR), F32)],
        compiler_params=_params(("parallel", "parallel", "parallel"), 40),
        name="dilated_attention",
    )(qkv, qkv, qkv, qkv, qkv)


def _forget_cumsum_kernel(f_ref, b_ref, c_ref, kb_ref, carry_ref):
    @pl.when(pl.program_id(1) == 0)
    def _():
        carry_ref[...] = jnp.zeros_like(carry_ref)

    blk = f_ref.shape[0]
    log_f = -_softplus(-(f_ref[...] + b_ref[...]))
    tri = (lax.broadcasted_iota(jnp.int32, (blk, blk), 0) >= lax.broadcasted_iota(jnp.int32, (blk, blk), 1))
    c = _dot_exact_lhs(tri.astype(BF16), log_f) + carry_ref[...]
    c_ref[...] = c
    carry_ref[...] = c[blk - 1:blk, :]

    width = kb_ref.shape[1]
    head = lax.broadcasted_iota(jnp.int32, (c.shape[1], width), 0)
    out = lax.broadcasted_iota(jnp.int32, (c.shape[1], width), 1)
    pair, lane = out // PAIR, out % PAIR
    placed = None
    for piece, part in enumerate(_split3(-c)):
        select = ((head == 2 * pair) & (lane == piece)) | ((head == 2 * pair + 1) & (lane == 3 + piece))
        term = _dot(part, select.astype(BF16))
        placed = term if placed is None else placed + term
    kb_ref[...] = placed.astype(BF16)


def forget_cumsum(fc, b_f, batch, seq, n_pairs, blk=512):
    w = fc.shape[1]
    nb = seq // blk
    row_block = lambda width: pl.BlockSpec((blk, width), lambda b, i: (b * nb + i, 0))
    return pl.pallas_call(
        _forget_cumsum_kernel,
        grid=(batch, nb),
        in_specs=[row_block(w), pl.BlockSpec((1, w), lambda b, i: (0, 0))],
        out_specs=[row_block(w), row_block(n_pairs * PAIR)],
        out_shape=[jax.ShapeDtypeStruct(fc.shape, F32), jax.ShapeDtypeStruct((fc.shape[0], n_pairs * PAIR), BF16)],
        scratch_shapes=[pltpu.VMEM((1, w), F32)],
        compiler_params=_params(("parallel", "arbitrary"), 16),
        name="forget_cumsum",
    )(fc, b_f)


FOX_ROWS = 32
FOX_PAIRS = 4


def _fox_kernel(qi_ref, kj_ref, q_ref, k_ref, v_ref, cb_ref, cq_ref, o_ref,
                m_ref, al_ref, cqr_ref, acc_ref, s_ref, p_ref):
    t = pl.program_id(2)
    i, j = qi_ref[t], kj_ref[t]
    tq, tk = q_ref.shape[0], k_ref.shape[0]
    n_tiles = tk // PAIR
    heads = range(2 * FOX_PAIRS)
    lane1 = lax.broadcasted_iota(jnp.int32, (1, PAIR), 1)
    head0 = lane1 < HEAD_DIM

    @pl.when(j == 0)
    def _():
        m_ref[...] = jnp.full_like(m_ref, NEG_INF)
        acc_ref[...] = jnp.zeros_like(acc_ref)
        c_blk = cq_ref[...]
        head_row = lax.broadcasted_iota(jnp.int32, (PAIR, PAIR), 0)
        for h in heads:
            pick = (head_row == pl.program_id(1) * len(heads) + h).astype(BF16)
            cqr_ref[h] = _dot_exact_rhs(c_blk, pick)

    def step(masked):
        lane = lax.broadcasted_iota(jnp.int32, (FOX_ROWS, PAIR), 1)
        sub = lax.broadcasted_iota(jnp.int32, (FOX_ROWS, PAIR), 0)
        for h in heads:
            g, e = divmod(h, 2)
            pair = pl.ds(g * PAIR, PAIR)
            q = q_ref[:, pair] * (HEAD_DIM ** -0.5)
            sel = head0 if e == 0 else jnp.logical_not(head0)
            bias_on = jnp.broadcast_to(((lane1 >= 3 * e) & (lane1 < 3 * e + 3)).astype(BF16), (tq, PAIR))
            s_ref[h] = _dot_nt(jnp.concatenate([jnp.where(sel, q, jnp.zeros_like(q)), bias_on], axis=1),
                               jnp.concatenate([k_ref[:, pair], cb_ref[:, pair]], axis=1))
        pv = []
        for h in heads:
            g, e = divmod(h, 2)
            sel = head0 if e == 0 else jnp.logical_not(head0)
            for c in range(tq // FOX_ROWS):
                rows = pl.ds(c * FOX_ROWS, FOX_ROWS)
                tiles = []
                for n in range(n_tiles):
                    s = s_ref[h, rows, pl.ds(n * PAIR, PAIR)]
                    if masked:
                        s = jnp.where(lane + n * PAIR <= sub + c * FOX_ROWS, s, NEG_INF)
                    tiles.append(s)
                mx = tiles[0]
                for s in tiles[1:]:
                    mx = jnp.maximum(mx, s)
                cq = cqr_ref[h, rows, :]
                m_prev = m_ref[h, rows, :]
                m_new = jnp.maximum(m_prev, jnp.max(mx, axis=-1, keepdims=True) + cq)
                shift = cq - m_new
                for n, s in enumerate(tiles):
                    p_ref[rows, pl.ds(h * tk + n * PAIR, PAIR)] = jnp.exp(s + shift).astype(BF16)
                m_ref[h, rows, :] = m_new
                al_ref[h, rows, :] = jnp.exp(m_prev - m_new)
            v = v_ref[:, pl.ds(g * PAIR, PAIR)]
            v_aug = jnp.concatenate([jnp.where(sel, v, jnp.zeros_like(v)),
                                     jnp.broadcast_to((lane1 == e).astype(BF16), (tk, PAIR))], axis=1)
            pv.append(_dot(p_ref[:, pl.ds(h * tk, tk)], v_aug))
        for g in range(FOX_PAIRS):
            a0, a1 = al_ref[2 * g], al_ref[2 * g + 1]
            alpha = jnp.concatenate([jnp.where(head0, a0, a1), jnp.where(lane1 == 0, a0, a1)], axis=1)
            acc = acc_ref[g] * alpha + pv[2 * g] + pv[2 * g + 1]
            if masked:
                den = jnp.where(head0, acc[:, PAIR:PAIR + 1], acc[:, PAIR + 1:PAIR + 2])
                o_ref[:, pl.ds(g * PAIR, PAIR)] = (acc[:, :PAIR] / den).astype(o_ref.dtype)
            else:
                acc_ref[g] = acc

    @pl.when(j < i)
    def _():
        step(False)

    @pl.when(j == i)
    def _():
        step(True)


def fox_attention(qkv, key_bias, c, batch, seq, n_pairs, q_col, k_col, v_col, tq):
    nq = seq // tq
    qi = jnp.asarray([i for i in range(nq) for _ in range(i + 1)], jnp.int32)
    kj = jnp.asarray([j for i in range(nq) for j in range(i + 1)], jnp.int32)
    gw, heads = FOX_PAIRS * PAIR, 2 * FOX_PAIRS
    assert all(x % FOX_PAIRS == 0 for x in (n_pairs, q_col, k_col, v_col)) and seq % tq == 0
    rep = pltpu.VMEM((heads, tq, PAIR), F32)
    grid_spec = pltpu.PrefetchScalarGridSpec(
        num_scalar_prefetch=2,
        grid=(batch, n_pairs // FOX_PAIRS, qi.shape[0]),
        in_specs=[pl.BlockSpec((tq, gw), lambda b, p, t, qi, kj: (b * nq + qi[t], q_col // FOX_PAIRS + p)),
                  pl.BlockSpec((tq, gw), lambda b, p, t, qi, kj: (b * nq + kj[t], k_col // FOX_PAIRS + p)),
                  pl.BlockSpec((tq, gw), lambda b, p, t, qi, kj: (b * nq + kj[t], v_col // FOX_PAIRS + p)),
                  pl.BlockSpec((tq, gw), lambda b, p, t, qi, kj: (b * nq + kj[t], p)),
                  pl.BlockSpec((tq, PAIR), lambda b, p, t, qi, kj: (b * nq + qi[t], 0))],
        out_specs=pl.BlockSpec((tq, gw), lambda b, p, t, qi, kj: (b * nq + qi[t], p)),
        scratch_shapes=[rep, rep, rep, pltpu.VMEM((FOX_PAIRS, tq, 2 * PAIR), F32),
                        pltpu.VMEM((heads, tq, tq), F32), pltpu.VMEM((tq, heads * tq), BF16)])
    return pl.pallas_call(
        _fox_kernel,
        grid_spec=grid_spec,
        out_shape=jax.ShapeDtypeStruct((batch * seq, n_pairs * PAIR), BF16),
        compiler_params=_params(("parallel", "parallel", "arbitrary"), 32),
        name="fox_attention",
    )(qi, kj, qkv, qkv, qkv, key_bias, c)


def _rwkv_prep_kernel(x_ref, xp_ref, xl_ref, xlp_ref, mix_ref, mixl_ref, w0_ref, w2_ref, a0_ref, a2_ref, g2_ref,
                      kk_ref, ka_ref, r_ref, lw_ref, k_ref, v_ref, a_ref, b_ref, g_ref, *, dim):
    first = pl.program_id(1) == 0
    row = lax.broadcasted_iota(jnp.int32, (x_ref.shape[0], 1), 0)

    def shift_lerp(cur_ref, prev_ref, mix):
        x = cur_ref[...]
        prev_row = jnp.where(first, 0.0, prev_ref[7:8, :])
        shifted = jnp.where(row == 0, prev_row, pltpu.roll(x, 1, 0))
        return x + (shifted - x) * mix

    x = shift_lerp(x_ref, xp_ref, mix_ref[...])
    xl = shift_lerp(xl_ref, xlp_ref, mixl_ref[...])
    r, k, v = x[:, 0:dim], x[:, dim:2 * dim], x[:, 2 * dim:3 * dim]
    xw, xa, xg = xl[:, 0:LORA_PAD], xl[:, LORA_PAD:2 * LORA_PAD], xl[:, 2 * LORA_PAD:]
    w_log = -_softplus(-(w0_ref[...] + _dot_x3(jnp.tanh(xw), w2_ref[...]))) - 0.5
    a = _sigmoid(a0_ref[...] + _dot_x3(xa, a2_ref[...]))
    g = _dot_x3(_sigmoid(xg), g2_ref[...])

    kk = k * kk_ref[...]
    ones = _head_sum_matrix().astype(BF16)
    norm = []
    for p in range(dim // PAIR):
        kp = kk[:, p * PAIR:(p + 1) * PAIR]
        norm.append(jnp.sqrt(_dot_exact_rhs(kp * kp, ones)))
    kk = kk / jnp.maximum(jnp.concatenate(norm, axis=1), 1e-12)

    r_ref[...] = r.astype(r_ref.dtype)
    lw_ref[...] = -jnp.exp(w_log)
    k_ref[...] = (k * (1.0 + (a - 1.0) * ka_ref[...])).astype(k_ref.dtype)
    v_ref[...] = v.astype(v_ref.dtype)
    a_ref[...] = (-kk).astype(a_ref.dtype)
    b_ref[...] = (kk * a).astype(b_ref.dtype)
    g_ref[...] = g.astype(g_ref.dtype)


def rwkv_prep(pa, pl_, mix, mix_l, w0, w2, a0, a2, g2, k_k, k_a, batch, seq, dim, tt=128):
    m = pa.shape[0]
    nt = seq // tt
    full = lambda arr: pl.BlockSpec(arr.shape, lambda b, i: (0, 0))
    cur = lambda arr: pl.BlockSpec((tt, arr.shape[1]), lambda b, i: (b * nt + i, 0))
    before = lambda arr: pl.BlockSpec((8, arr.shape[1]),
                                      lambda b, i: (jnp.maximum((b * nt + i) * (tt // 8) - 1, 0), 0))
    out_spec = pl.BlockSpec((tt, dim), lambda b, i: (b * nt + i, 0))
    consts = [mix, mix_l, w0, w2, a0, a2, g2, k_k, k_a]
    return pl.pallas_call(
        functools.partial(_rwkv_prep_kernel, dim=dim),
        grid=(batch, nt),
        in_specs=[cur(pa), before(pa), cur(pl_), before(pl_)] + [full(c) for c in consts],
        out_specs=[out_spec] * 7,
        out_shape=[jax.ShapeDtypeStruct((m, dim), F32 if name == "lw" else BF16)
                   for name in ("r", "lw", "k", "v", "a", "b", "g")],
        compiler_params=_params(("parallel", "parallel"), 48),
        name="rwkv_prep",
    )(pa, pa, pl_, pl_, *consts)


def _stack_heads(x, head0):
    zero = jnp.zeros_like(x)
    return jnp.concatenate([jnp.where(head0, x, zero), jnp.where(head0, zero, x)], axis=0)


def _rwkv_scan_kernel(r_ref, lw_ref, k_ref, v_ref, a_ref, b_ref, g_ref, rk_ref, lnw_ref, lnb_ref,
                      o_ref, s_ref, y_ref):
    c_len = RWKV_CHUNK
    tt = r_ref.shape[0]
    chunks = range(tt // c_len)

    @pl.when(pl.program_id(2) == 0)
    def _():
        s_ref[...] = jnp.zeros_like(s_ref)

    head0 = lax.broadcasted_iota(jnp.int32, (1, PAIR), 1) < HEAD_DIM
    n2 = 2 * c_len
    row = lax.broadcasted_iota(jnp.int32, (n2, n2), 0)
    col = lax.broadcasted_iota(jnp.int32, (n2, n2), 1)
    t_loc, j_loc = row % c_len, col % c_len
    same_head = (row // c_len) == (col // c_len)
    strict = same_head & (j_loc < t_loc)
    incl = same_head & (j_loc <= t_loc)
    incl2 = jnp.concatenate([incl, incl], axis=1)
    sub_block = same_head & ((t_loc // 16) == (j_loc // 16))
    eye = (row == col).astype(F32)
    zeros = jnp.zeros((n2, PAIR), F32)
    tri = (lax.broadcasted_iota(jnp.int32, (c_len, c_len), 0)
           >= lax.broadcasted_iota(jnp.int32, (c_len, c_len), 1)).astype(BF16)

    def rows(x, c):
        return x[c * c_len:(c + 1) * c_len]

    pairs = range(r_ref.shape[1] // PAIR)
    units = [(g, c) for c in chunks for g in pairs]

    def load(ref, g):
        return ref[:, pl.ds(g * PAIR, PAIR)]

    lw = [load(lw_ref, g) for g in pairs]
    r, k, v, a, b = ([load(ref, g).astype(F32) for g in pairs] for ref in (r_ref, k_ref, v_ref, a_ref, b_ref))
    cw = [jnp.concatenate([_dot_exact_lhs(tri, rows(lw[g], c)) for c in chunks], axis=0) for g in pairs]
    e_pos, e_neg = [jnp.exp(x) for x in cw], [jnp.exp(-x) for x in cw]
    a_t = [a[g] * jnp.exp(cw[g] - lw[g]) for g in pairs]
    r_t = [r[g] * e_pos[g] for g in pairs]
    b_t = [b[g] * e_neg[g] for g in pairs]
    k_t = [k[g] * e_neg[g] for g in pairs]

    e_last = [jnp.exp(rows(cw[g], c)[c_len - 1:c_len, :]) for g, c in units]
    a_s = [_stack_heads(rows(a_t[g], c), head0) for g, c in units]
    r_s = [_stack_heads(rows(r_t[g], c), head0) for g, c in units]
    v_s = [_stack_heads(rows(v[g], c), head0) for g, c in units]
    scores = [_dot_nt(jnp.concatenate([a_s[u], r_s[u]], axis=0).astype(BF16),
                      jnp.concatenate([_stack_heads(rows(b_t[g], c), head0),
                                       _stack_heads(rows(k_t[g], c), head0)], axis=0).astype(BF16))
              for u, (g, c) in enumerate(units)]
    l_ab = [jnp.where(strict, s[:n2, :n2], 0.0) for s in scores]

    d1 = [jnp.where(sub_block, x, 0.0) for x in l_ab]
    d2 = [_dot_bf16(x, x) for x in d1]
    d4 = [_dot_bf16(x, x) for x in d2]
    d8 = [_dot_bf16(x, x) for x in d4]
    t_lo = [_dot_bf16(eye + x, eye + y) for x, y in zip(d1, d2)]
    t_hi = [_dot_bf16(eye + x, eye + y) for x, y in zip(d4, d8)]
    t_diag = [_dot_bf16(x, y) for x, y in zip(t_lo, t_hi)]
    n1 = [_dot_bf16(t, x - d) for t, x, d in zip(t_diag, l_ab, d1)]
    n_sq = [_dot_bf16(x, x) for x in n1]
    t_rest = [_dot_bf16(eye + x, eye + y) for x, y in zip(n1, n_sq)]
    t_inv = [_dot_bf16(x, t) for x, t in zip(t_rest, t_diag)]

    lv = [_dot_bf16(jnp.where(strict, s[:n2, n2:], 0.0), vs) for s, vs in zip(scores, v_s)]
    au = [_dot_bf16(t, jnp.concatenate([x, y], axis=1)) for t, x, y in zip(t_inv, a_s, lv)]
    z = []
    for u, (g, c) in enumerate(units):
        e_end = e_last[u] * rows(e_neg[g], c)
        bend_s = _stack_heads(rows(b[g], c) * e_end, head0)
        kend_s = _stack_heads(rows(k[g], c) * e_end, head0)
        lhs = jnp.concatenate([jnp.where(incl2, scores[u][n2:], 0.0),
                               jnp.concatenate([bend_s.T, kend_s.T], axis=1)], axis=0)
        rhs = jnp.concatenate([au[u], jnp.concatenate([zeros, v_s[u]], axis=1)], axis=0)
        z.append(_dot_bf16(lhs, rhs))

    state = [s_ref[g] for g in pairs]
    for u, (g, c) in enumerate(units):
        r_hat = r_s[u] + z[u][:n2, :PAIR]
        y_hat = z[u][:n2, PAIR:]
        lhs = jnp.concatenate([r_hat[:c_len] + r_hat[c_len:], z[u][n2:, :PAIR] + eye * e_last[u]], axis=0)
        prod = _dot_bf16(lhs, state[g])
        y_ref[pl.ds(c * c_len, c_len), pl.ds(g * PAIR, PAIR)] = prod[:c_len] + y_hat[:c_len] + y_hat[c_len:]
        state[g] = prod[c_len:] + z[u][n2:, PAIR:]
    for g in pairs:
        s_ref[g] = state[g]

    ones = _head_sum_matrix().astype(BF16)
    for g in pairs:
        y = load(y_ref, g)
        mu = _dot_exact_rhs(y, ones) * (1.0 / HEAD_DIM)
        d = y - mu
        var = _dot_exact_rhs(d * d, ones) * (1.0 / HEAD_DIM)
        yn = d * lax.rsqrt(var + RWKV_LN_EPS) * load(lnw_ref, g) + load(lnb_ref, g)
        bonus = _dot_exact_rhs(r[g] * k[g] * load(rk_ref, g), ones) * v[g]
        o_ref[:, pl.ds(g * PAIR, PAIR)] = ((yn + bonus) * load(g_ref, g)).astype(o_ref.dtype)


def rwkv_scan(r, lw, k, v, a, b, g, r_k, ln_w, ln_b, batch, seq, tt=512):
    m, dim = r.shape
    width = RWKV_PAIRS * PAIR
    assert dim % width == 0 and seq % tt == 0, (dim, seq)
    nt = seq // tt
    seq_spec = pl.BlockSpec((tt, width), lambda bb, p, i: (bb * nt + i, p))
    vec_spec = pl.BlockSpec((1, width), lambda bb, p, i: (0, p))
    return pl.pallas_call(
        _rwkv_scan_kernel,
        grid=(batch, dim // width, nt),
        in_specs=[seq_spec] * 7 + [vec_spec] * 3,
        out_specs=seq_spec,
        out_shape=jax.ShapeDtypeStruct((m, dim), BF16),
        scratch_shapes=[pltpu.VMEM((RWKV_PAIRS, PAIR, PAIR), F32), pltpu.VMEM((tt, width), F32)],
        compiler_params=_params(("parallel", "parallel", "arbitrary"), 32),
        name="rwkv_scan",
    )(r, lw, k, v, a, b, g, r_k, ln_w, ln_b)


def _pad_cols(w, width):
    return jnp.pad(w, ((0, 0), (0, width - w.shape[1])))


def _pad_rows(w, height):
    return jnp.pad(w, ((0, height - w.shape[0]), (0, 0)))


def _dup_kv_heads(w):
    d, c = w.shape
    return jnp.tile(w.reshape(d, c // HEAD_DIM, 1, HEAD_DIM), (1, 1, 2, 1)).reshape(d, 2 * c)


def even_mixer(h, xn, ss, next_gain, batch, seq, w_in, mix, w0, w2, a0, a2, g2, k_k, k_a, r_k, ln_w, ln_b, sink, w_out,
               layer, dim, swa_heads, swa_kv_heads, tm, tn):
    d_model = xn.shape[1]
    decay_lora, aaa_lora = w2.shape[0], a2.shape[0]
    o = 3 * dim
    o_a, o_g = o + decay_lora, o + decay_lora + aaa_lora
    rwkv_cols = o_g + g2.shape[0]
    scale = HEAD_DIM ** -0.5

    def lora_layout(t):
        return jnp.concatenate([_pad_cols(t[..., o:o_a], LORA_PAD), _pad_cols(t[..., o_a:o_g], LORA_PAD),
                                t[..., o_g:rwkv_cols]], axis=-1)

    part = lambda lo, hi: w_in[layer, :, lo:hi]
    w_l = jnp.concatenate([_pad_cols(part(o, o_a), LORA_PAD), _pad_cols(part(o_a, o_g), LORA_PAD),
                           part(o_g, rwkv_cols)], axis=1)
    q_end = rwkv_cols + swa_heads * HEAD_DIM
    k_end = q_end + swa_kv_heads * HEAD_DIM
    w_b = jnp.concatenate([part(rwkv_cols, q_end) * scale, _dup_kv_heads(part(q_end, k_end)),
                           _dup_kv_heads(part(k_end, w_in.shape[2]))], axis=1)

    pa = matmul_sum([(xn, d_model, 0)], [(w_in, d_model, 0)], o, F32, tm, tn, row_ss=ss, layer=layer)
    pa_l = matmul_sum([(xn, d_model, 0)], [(w_l, d_model, 0)], w_l.shape[1], F32, tm, tn, row_ss=ss)
    pb = matmul_sum([(xn, d_model, 0)], [(w_b, d_model, 0)], w_b.shape[1], BF16, tm, tn, row_ss=ss)

    row = lambda t: t.reshape(1, -1)
    prep = rwkv_prep(pa, pa_l, row(mix)[:, :o], lora_layout(row(mix)), row(w0), _pad_rows(w2, LORA_PAD), row(a0),
                     _pad_rows(a2, LORA_PAD), g2, row(k_k), row(k_a), batch, seq, dim)
    y_a = rwkv_scan(*prep, row(r_k), row(ln_w), row(ln_b), batch, seq)

    q_pairs, kv_pairs = swa_heads // 2, swa_kv_heads
    kv_w = kv_pairs * PAIR
    kv_base = (q_pairs * PAIR) // kv_w
    sink_lanes = jnp.repeat(sink, HEAD_DIM).reshape(1, -1)
    y_b = swa_attention(pb, 0, kv_base, kv_base + 1, batch, seq, q_pairs, kv_pairs, SWA_WINDOW - 1, sink_lanes)

    return _out_proj(h, y_a, y_b, w_out, layer, next_gain, tm, tn)


def odd_mixer(h, xn, ss, next_gain, batch, seq, w_in, b_f, w_out, layer, fox_heads, dil_heads, tm, tn, fox_tq):
    d_model = xn.shape[1]
    cw, dw = fox_heads * HEAD_DIM, dil_heads * HEAD_DIM
    scale = HEAD_DIM ** -0.5
    f0 = 3 * cw
    d0 = f0 + fox_heads
    part = lambda lo, hi: w_in[layer, :, lo:hi]
    w_d = jnp.concatenate([part(d0, d0 + dw) * scale, part(d0 + dw, w_in.shape[2])], axis=1)
    w_f = _pad_cols(part(f0, d0), PAIR)
    proj = matmul_sum([(xn, d_model, 0)], [(w_in, d_model, 0)], f0, BF16, tm, tn, row_ss=ss, layer=layer)
    proj_d = matmul_sum([(xn, d_model, 0)], [(w_d, d_model, 0)], 3 * dw, F32, tm, tn, row_ss=ss)
    fc = matmul_sum([(xn, d_model, 0)], [(w_f, d_model, 0)], PAIR, F32, tm, PAIR, row_ss=ss)

    fox_pairs = fox_heads // 2
    c, key_bias = forget_cumsum(fc, _pad_cols(b_f.reshape(1, -1), PAIR), batch, seq, fox_pairs)
    y_c = fox_attention(proj, key_bias, c, batch, seq, fox_pairs, 0, fox_pairs, 2 * fox_pairs, fox_tq)

    y_d = dilated_attention(proj_d, batch, seq, dil_heads // 2)

    return _out_proj(h, y_c, y_d, w_out, layer, next_gain, tm, tn)


def _out_proj(h, y_1, y_2, w_out, layer, next_gain, tm, tn):
    w1, w2 = y_1.shape[1], y_2.shape[1]
    kw = math.gcd(w1, w2)
    n1, n2 = w1 // kw, w2 // kw
    a_ops = [(y_1, kw, c) for c in range(n1)] + [(y_2, kw, c) for c in range(n2)]
    b_ops = [(w_out, kw, c) for c in range(n1 + n2)]
    return matmul_sum(a_ops, b_ops, w_out.shape[2], F32, tm, tn, residual=h, next_gain=next_gain, layer=layer)


def ffn(h, xn, ss, next_gain, w_gate, w_up, w_down, layer, tm, tn_hidden, tm_down, tn_down):
    half = w_gate.shape[2] // 2
    hidden = swiglu_hidden(xn, ss, w_gate, w_up, layer, tm, tn_hidden)
    return matmul_sum([(hidden, half, 0), (hidden, half, 1)], [(w_down, half, 0), (w_down, half, 1)],
                      w_down.shape[2], F32, tm_down, tn_down, residual=h, next_gain=next_gain, layer=layer,
                      a_buffers=1, vmem_mib=56)


def kernel(x, norm_mix, norm_ffn, norm_final, ffn_w_gate, ffn_w_up, ffn_w_down, ev_w_in, rwkv_mix, rwkv_w0, rwkv_w2, rwkv_a0, rwkv_a2, rwkv_g2, rwkv_k_k, rwkv_k_a, rwkv_r_k, rwkv_ln_w, rwkv_ln_b, swa_sink, ev_w_out, od_w_in, fox_b_f, od_w_out):
    batch, seq, d_model = x.shape
    depth = norm_mix.shape[0]
    dim = rwkv_w0.shape[1]
    swa_heads = swa_sink.shape[1]
    fox_heads = fox_b_f.shape[1]
    dil_heads = d_model // HEAD_DIM - fox_heads
    tm, tn = 1024, 512
    w_gate, w_up, w_down = ffn_w_gate.astype(BF16), ffn_w_up.astype(BF16), ffn_w_down.astype(BF16)
    w_out_even, w_out_odd = ev_w_out.astype(BF16), od_w_out.astype(BF16)
    w_in_even, w_in_odd = ev_w_in.astype(BF16), od_w_in.astype(BF16)
    h = x.reshape(batch * seq, d_model)
    xn, ss = split_norm(h, norm_mix[0])
    for layer in range(depth):
        i = layer // 2
        gain_ffn = norm_ffn[layer].reshape(1, d_model)
        if layer % 2 == 0:
            h, xn, ss = even_mixer(h, xn, ss, gain_ffn, batch, seq, w_in_even, rwkv_mix[i], rwkv_w0[i], rwkv_w2[i],
                                   rwkv_a0[i], rwkv_a2[i], rwkv_g2[i], rwkv_k_k[i], rwkv_k_a[i], rwkv_r_k[i].reshape(-1),
                                   rwkv_ln_w[i], rwkv_ln_b[i], swa_sink[i], w_out_even, i, dim, swa_heads,
                                   swa_heads // 8, tm, tn)
        else:
            h, xn, ss = odd_mixer(h, xn, ss, gain_ffn, batch, seq, w_in_odd, fox_b_f[i], w_out_odd, i, fox_heads,
                                  dil_heads, tm, tn, 512)
        if layer + 1 < depth:
            h, xn, ss = ffn(h, xn, ss, norm_mix[layer + 1].reshape(1, d_model), w_gate, w_up, w_down, layer,
                            tm, 256, tm, 256)
        else:
            h = ffn(h, xn, ss, None, w_gate, w_up, w_down, layer, tm, 256, tm, 256)
    return rmsnorm(h, norm_final, F32).reshape(batch, seq, d_model)
```
